```python
import jax, jax.numpy as jnp
from jax import lax
import numpy as np


D_MODEL = 1024
BATCH = 8
SEQ = 2048
DEPTH = 4

GRID_W = 64
CTX_LEN = 256
N_HEADS = 8
QK_NOPE = 64
QK_ROPE = 32
V_DIM = 64
Q_LORA = 256
KV_LORA = 128
ROPE_BASE = 10000.0
Q_BLOCK = 128
SGU_DIM = 256
SGU_HEADS = 4
CHUNK = 128
POOL_DIM = 256
POOL_WINDOWS = (2, 4, 8, 16)
N_POOL = len(POOL_WINDOWS)
POOL_GDIM = POOL_DIM // N_POOL
FOURIER_DIM = 256
FOURIER_HEADS = 4
ATTN_DIM = N_HEADS * V_DIM
MIX_DIM = ATTN_DIM + SGU_DIM + POOL_DIM + FOURIER_DIM
OFF_Q = 0
OFF_KV = OFF_Q + Q_LORA
OFF_KR = OFF_KV + KV_LORA
OFF_SGU = OFF_KR + QK_ROPE
OFF_POOL = OFF_SGU + 2 * SGU_DIM
OFF_FOURIER = OFF_POOL + POOL_DIM
IN_DIM = OFF_FOURIER + FOURIER_DIM
D_FF = ((8 * D_MODEL + 3 * 256 - 1) // (3 * 256)) * 256
LN_EPS = 1e-6
DEEPNORM_ALPHA = (2.0 * DEPTH) ** 0.25
DEEPNORM_BETA = (8.0 * DEPTH) ** -0.25

kernel_name = 'hybrid_parallel_mixer_dit'


def layer_norm(x, g, b):
    xf = x.astype(jnp.float32)
    mu = jnp.mean(xf, axis=-1, keepdims=True)
    var = jnp.mean(jnp.square(xf - mu), axis=-1, keepdims=True)
    return ((xf - mu) * lax.rsqrt(var + LN_EPS)).astype(x.dtype) * g + b


def rms_norm(x, g):
    xf = x.astype(jnp.float32)
    return (xf * lax.rsqrt(jnp.mean(jnp.square(xf), axis=-1, keepdims=True) + LN_EPS)).astype(x.dtype) * g


def axial_rope_angles(length):
    rows = length // GRID_W
    row = jnp.repeat(jnp.arange(rows), GRID_W).astype(jnp.float32)
    col = jnp.tile(jnp.arange(GRID_W), rows).astype(jnp.float32)
    n_freq = QK_ROPE // 4
    inv_freq = ROPE_BASE ** (-jnp.arange(n_freq, dtype=jnp.float32) / n_freq)
    return row[:, None] * inv_freq, col[:, None] * inv_freq


def rotate(x, ang):
    k = x.shape[-1] // 2
    x1, x2 = x[..., :k], x[..., k:]
    cos = jnp.cos(ang)[None, :, None, :].astype(x.dtype)
    sin = jnp.sin(ang)[None, :, None, :].astype(x.dtype)
    return jnp.concatenate([x1 * cos - x2 * sin, x1 * sin + x2 * cos], axis=-1)


def axial_rope(x, angles):
    ang_r, ang_c = angles
    half = x.shape[-1] // 2
    return jnp.concatenate([rotate(x[..., :half], ang_r), rotate(x[..., half:], ang_c)], axis=-1)


def mla_qkv(proj, q_norm, w_uq, kv_norm, w_uk, w_uv, angles):
    B, L, _ = proj.shape
    cq = rms_norm(proj[..., OFF_Q:OFF_KV], q_norm)
    ckv = rms_norm(proj[..., OFF_KV:OFF_KR], kv_norm)
    k_rope = proj[..., OFF_KR:OFF_SGU][:, :, None, :]
    q = (cq @ w_uq).reshape(B, L, N_HEADS, QK_NOPE + QK_ROPE)
    k_nope = (ckv @ w_uk).reshape(B, L, N_HEADS, QK_NOPE)
    v = (ckv @ w_uv).reshape(B, L, N_HEADS, V_DIM)
    q_nope, q_rope = q[..., :QK_NOPE], q[..., QK_NOPE:]
    if angles is not None:
        q_rope = axial_rope(q_rope, angles)
        k_rope = axial_rope(k_rope, angles)
    q = jnp.concatenate([q_nope, q_rope], axis=-1)
    k = jnp.concatenate([k_nope, jnp.broadcast_to(k_rope, (B, L, N_HEADS, QK_ROPE))], axis=-1)
    return q, k, v


def attend(q, k, v):
    s = jnp.einsum('bqhd,bkhd->bhqk', q, k).astype(jnp.float32) * (QK_NOPE + QK_ROPE) ** -0.5
    p = jax.nn.softmax(s, axis=-1).astype(v.dtype)
    return jnp.einsum('bhqk,bkhd->bqhd', p, v)


def blocked_attend(q, k, v):
    B, L, H, d = q.shape
    nb = L // Q_BLOCK
    qb = q.reshape(B, nb, Q_BLOCK, H, d).transpose(1, 0, 2, 3, 4)
    ob = lax.map(lambda qi: attend(qi, k, v), qb)
    return ob.transpose(1, 0, 2, 3, 4).reshape(B, L, H * V_DIM)


def spatial_gating(uv, g, b, w_s, b_s):
    B, L, _ = uv.shape
    u, v = uv[..., :SGU_DIM], uv[..., SGU_DIM:]
    v = layer_norm(v, g, b).reshape(B, L // CHUNK, CHUNK, SGU_HEADS, SGU_DIM // SGU_HEADS)
    mixed = jnp.einsum('gpq,bnqgc->bnpgc', w_s, v) + b_s.T[:, :, None]
    return u * mixed.reshape(B, L, SGU_DIM)


def multiscale_pool(p, w_pool, pool_scale):
    B, L, _ = p.shape
    cs = jnp.concatenate([jnp.zeros((B, 1, POOL_DIM), jnp.float32),
                          jnp.cumsum(p.astype(jnp.float32), axis=1)], axis=1)
    t = jnp.arange(L)
    outs = []
    for gi, w in enumerate(POOL_WINDOWS):
        lo = jnp.clip(t - w // 2, 0, L)
        hi = jnp.clip(t + w // 2, 0, L)
        seg = cs[:, :, gi * POOL_GDIM:(gi + 1) * POOL_GDIM]
        mean = (seg[:, hi] - seg[:, lo]) / (hi - lo).astype(jnp.float32)[None, :, None]
        tok = p[..., gi * POOL_GDIM:(gi + 1) * POOL_GDIM]
        outs.append((mean.astype(p.dtype) - tok) @ w_pool[gi])
    return jnp.concatenate(outs, axis=-1) * pool_scale


def fourier_mix(f, w_f):
    B, L, _ = f.shape
    fh = f.astype(jnp.float32).reshape(B, L, FOURIER_HEADS, FOURIER_DIM // FOURIER_HEADS)
    spec = jnp.fft.fft2(fh, axes=(1, 3), norm='ortho').real
    return spec.astype(f.dtype).reshape(B, L, FOURIER_DIM) @ w_f


def local_mixers(proj, sgu_g, sgu_b, w_s, b_s, w_pool, pool_scale, w_f):
    return jnp.concatenate([
        spatial_gating(proj[..., OFF_SGU:OFF_POOL], sgu_g, sgu_b, w_s, b_s),
        multiscale_pool(proj[..., OFF_POOL:OFF_FOURIER], w_pool, pool_scale),
        fourier_mix(proj[..., OFF_FOURIER:IN_DIM], w_f),
    ], axis=-1)


def residual_tail(x, mix, g1, sh2, sc2, g2, w_out, ln1g, ln1b, w1, w3, w2, ln2g, ln2b):
    x = layer_norm(DEEPNORM_ALPHA * x + g1 * (mix @ w_out), ln1g, ln1b)
    h = x * (1.0 + sc2) + sh2
    ffn = (jax.nn.silu(h @ w1) * (h @ w3)) @ w2
    return layer_norm(DEEPNORM_ALPHA * x + g2 * ffn, ln2g, ln2b)


def setup_inputs(seed: int = 0) -> dict:
    key = jax.random.key(seed)
    ks = jax.random.split(key, 32)
    f32 = jnp.float32

    def nrm(k, shape, scale):
        return jax.random.normal(k, shape, f32) * scale

    def gain(k, shape):
        return 1.0 + 0.02 * jax.random.normal(k, shape, f32)

    beta = DEEPNORM_BETA
    return {
        'x': nrm(ks[0], (BATCH, SEQ, D_MODEL), 1.0),
        'c': nrm(ks[1], (BATCH, D_MODEL), 1.0),
        'ctx': nrm(ks[2], (BATCH, CTX_LEN, D_MODEL), 1.0),
        'c_ctx': nrm(ks[3], (D_MODEL,), 1.0),
        'w_mod': nrm(ks[4], (DEPTH, D_MODEL, 6 * D_MODEL), 0.5 * D_MODEL ** -0.5),
        'b_mod': nrm(ks[5], (DEPTH, 6 * D_MODEL), 0.02),
        'w_in': nrm(ks[6], (DEPTH, D_MODEL, IN_DIM), D_MODEL ** -0.5),
        'q_norm': gain(ks[7], (DEPTH, Q_LORA)),
        'w_uq': nrm(ks[8], (DEPTH, Q_LORA, N_HEADS * (QK_NOPE + QK_ROPE)), Q_LORA ** -0.5),
        'kv_norm': gain(ks[9], (DEPTH, KV_LORA)),
        'w_uk': nrm(ks[10], (DEPTH, KV_LORA, N_HEADS * QK_NOPE), KV_LORA ** -0.5),
        'w_uv': nrm(ks[11], (DEPTH, KV_LORA, N_HEADS * V_DIM), KV_LORA ** -0.5),
        'sgu_ln_g': gain(ks[12], (DEPTH, SGU_DIM)),
        'sgu_ln_b': nrm(ks[13], (DEPTH, SGU_DIM), 0.02),
        'w_spatial': nrm(ks[14], (DEPTH, SGU_HEADS, CHUNK, CHUNK), CHUNK ** -0.5),
        'b_spatial': gain(ks[15], (DEPTH, SGU_HEADS, CHUNK)),
        'w_pool': nrm(ks[16], (DEPTH, N_POOL, POOL_GDIM, POOL_GDIM), POOL_GDIM ** -0.5),
        'pool_scale': gain(ks[17], (DEPTH, POOL_DIM)),
        'w_fourier': nrm(ks[18], (DEPTH, FOURIER_DIM, FOURIER_DIM), FOURIER_DIM ** -0.5),
        'w_out': nrm(ks[19], (DEPTH, MIX_DIM, D_MODEL), beta * MIX_DIM ** -0.5),
        'ln1_g': gain(ks[20], (DEPTH, D_MODEL)),
        'ln1_b': nrm(ks[21], (DEPTH, D_MODEL), 0.02),
        'w_ffn1': nrm(ks[22], (DEPTH, D_MODEL, D_FF), D_MODEL ** -0.5),
        'w_ffn3': nrm(ks[23], (DEPTH, D_MODEL, D_FF), D_MODEL ** -0.5),
        'w_ffn2': nrm(ks[24], (DEPTH, D_FF, D_MODEL), beta * D_FF ** -0.5),
        'ln2_g': gain(ks[25], (DEPTH, D_MODEL)),
        'ln2_b': nrm(ks[26], (DEPTH, D_MODEL), 0.02),
    }


def reference(x, c, ctx, c_ctx, w_mod, b_mod, w_in, q_norm, w_uq, kv_norm, w_uk, w_uv,
              sgu_ln_g, sgu_ln_b, w_spatial, b_spatial, w_pool, pool_scale, w_fourier,
              w_out, ln1_g, ln1_b, w_ffn1, w_ffn3, w_ffn2, ln2_g, ln2_b):
    B, L, _ = x.shape
    angles = axial_rope_angles(L)
    silu_c = jax.nn.silu(c)
    silu_cc = jax.nn.silu(c_ctx)
    x_ctx = ctx
    for l in range(DEPTH):
        last = l == DEPTH - 1
        mod = silu_c @ w_mod[l] + b_mod[l]
        sh1, sc1, g1, sh2, sc2, g2 = [m[:, None, :] for m in jnp.split(mod, 6, axis=-1)]
        mod_c = silu_cc @ w_mod[l] + b_mod[l]
        sh1c, sc1c, g1c, sh2c, sc2c, g2c = jnp.split(mod_c, 6, axis=-1)

        h = x * (1.0 + sc1) + sh1
        h_c = x_ctx * (1.0 + sc1c) + sh1c
        proj = h @ w_in[l]
        proj_c = h_c @ w_in[l]

        q, k, v = mla_qkv(proj, q_norm[l], w_uq[l], kv_norm[l], w_uk[l], w_uv[l], angles)
        q_c, k_c, v_c = mla_qkv(proj_c, q_norm[l], w_uq[l], kv_norm[l], w_uk[l], w_uv[l], None)

        attn = blocked_attend(q, jnp.concatenate([k_c, k], axis=1), jnp.concatenate([v_c, v], axis=1))
        mix = jnp.concatenate([attn, local_mixers(proj, sgu_ln_g[l], sgu_ln_b[l], w_spatial[l], b_spatial[l],
                                                  w_pool[l], pool_scale[l], w_fourier[l])], axis=-1)
        x_new = residual_tail(x, mix, g1, sh2, sc2, g2, w_out[l], ln1_g[l], ln1_b[l],
                              w_ffn1[l], w_ffn3[l], w_ffn2[l], ln2_g[l], ln2_b[l])

        if not last:
            attn_c = attend(q_c, k_c, v_c).reshape(B, x_ctx.shape[1], ATTN_DIM)
            mix_c = jnp.concatenate([attn_c, local_mixers(proj_c, sgu_ln_g[l], sgu_ln_b[l], w_spatial[l],
                                                          b_spatial[l], w_pool[l], pool_scale[l],
                                                          w_fourier[l])], axis=-1)
            x_ctx = residual_tail(x_ctx, mix_c, g1c, sh2c, sc2c, g2c, w_out[l], ln1_g[l], ln1_b[l],
                                  w_ffn1[l], w_ffn3[l], w_ffn2[l], ln2_g[l], ln2_b[l])
        x = x_new
    return x
```

```python
import functools

import numpy as np
import jax
import jax.numpy as jnp
from jax import lax
from jax.experimental import pallas as pl
from jax.experimental.pallas import tpu as pltpu

F32 = jnp.float32
BF16 = jnp.bfloat16

D_MODEL = 1024
DEPTH = 4
GRID_W = 64
N_HEADS = 8
QK_NOPE = 64
QK_ROPE = 32
V_DIM = 64
Q_LORA = 256
KV_LORA = 128
ROPE_BASE = 10000.0
SGU_DIM = 256
SGU_HEADS = 4
CHUNK = 128
POOL_DIM = 256
POOL_WINDOWS = (2, 4, 8, 16)
POOL_GDIM = POOL_DIM // len(POOL_WINDOWS)
FOURIER_DIM = 256
FOURIER_HEADS = 4
ATTN_DIM = N_HEADS * V_DIM
LOC_DIM = SGU_DIM + POOL_DIM + FOURIER_DIM
OFF_SGU = Q_LORA + KV_LORA + QK_ROPE
IN_DIM = OFF_SGU + 2 * SGU_DIM + POOL_DIM + FOURIER_DIM
D_FF = ((8 * D_MODEL + 3 * 256 - 1) // (3 * 256)) * 256
LN_EPS = 1e-6
DEEPNORM_ALPHA = (2.0 * DEPTH) ** 0.25
QK_SCALE = (QK_NOPE + QK_ROPE) ** -0.5
SGU_GRP_SHIFT = (SGU_DIM // SGU_HEADS).bit_length() - 1
POOL_GRP_SHIFT = POOL_GDIM.bit_length() - 1

LANE = 128
HEAD_SLAB = LANE
PROJ_A = 512
PROJ_W = PROJ_A + 2 * SGU_DIM + POOL_DIM + FOURIER_DIM
POOL_PAD = 16
VMEM_LIMIT = 56 * 1024 * 1024

TM = 512
TQ = 512
MOD_NB = 1536
MOD_ROWS = 16


def _resident(shape):
    n = len(shape)
    return pl.BlockSpec(shape, lambda *_: (0,) * n, pipeline_mode=pl.Buffered(1))


def _dot(a, b):
    return jnp.dot(a, b, preferred_element_type=F32)


def _dot_nt(a, b):
    return lax.dot_general(a, b, (((1,), (1,)), ((), ())), preferred_element_type=F32)


def _layer_norm(x, g, b):
    mu = jnp.mean(x, axis=-1, keepdims=True)
    xc = x - mu
    var = jnp.mean(xc * xc, axis=-1, keepdims=True)
    return xc * lax.rsqrt(var + LN_EPS) * g + b


def _rms_norm(x, g):
    return x * lax.rsqrt(jnp.mean(x * x, axis=-1, keepdims=True) + LN_EPS) * g


def _mod_kernel(c_ref, w_ref, b_ref, o_ref):
    c = c_ref[...]
    s = c * (1.0 / (1.0 + jnp.exp(-c)))
    o_ref[0] = _dot(s.astype(BF16), w_ref[0].astype(BF16)) + b_ref[0]


def _modulation(cc, w_mod, b_mod):
    d6 = w_mod.shape[-1]
    return pl.pallas_call(
        _mod_kernel,
        grid=(DEPTH, d6 // MOD_NB),
        in_specs=[
            pl.BlockSpec((MOD_ROWS, D_MODEL), lambda l, n: (0, 0)),
            pl.BlockSpec((1, D_MODEL, MOD_NB), lambda l, n: (l, 0, n)),
            pl.BlockSpec((1, 1, MOD_NB), lambda l, n: (l, 0, n)),
        ],
        out_specs=pl.BlockSpec((1, MOD_ROWS, MOD_NB), lambda l, n: (l, 0, n)),
        out_shape=jax.ShapeDtypeStruct((DEPTH, MOD_ROWS, d6), F32),
        compiler_params=pltpu.CompilerParams(dimension_semantics=("arbitrary", "arbitrary")),
        name="modulation",
    )(cc, w_mod, b_mod.reshape(DEPTH, 1, d6))


def _inproj_kernel(x_ref, mod_ref, cos_ref, sin_ref, win_ref, qn_ref, kvn_ref, wq_ref, wqs_ref,
                   wk_ref, wks_ref, wuvt_ref, q_ref, k_ref, vt_ref, loc_ref, pool_ref):
    x = x_ref[...]
    sh1 = mod_ref[0, :, 0:D_MODEL]
    sc1 = mod_ref[0, :, D_MODEL:2 * D_MODEL]
    h = (x * (1.0 + sc1) + sh1).astype(BF16)
    proj = _dot(h, win_ref[...])
    cqn = _rms_norm(proj[:, 0:Q_LORA], qn_ref[...]).astype(BF16)
    ckvn = _rms_norm(proj[:, Q_LORA:Q_LORA + KV_LORA], kvn_ref[...]).astype(BF16)
    xk = jnp.concatenate([ckvn, proj[:, Q_LORA + KV_LORA:PROJ_A].astype(BF16)], axis=1)
    q = _dot(cqn, wq_ref[...])
    qs = _dot(cqn, wqs_ref[...])
    k = _dot(xk, wk_ref[...])
    ks = _dot(xk, wks_ref[...])
    cos = cos_ref[...]
    sin = sin_ref[...]
    cos_q = cos * QK_SCALE
    sin_q = sin * QK_SCALE
    for hd in range(N_HEADS):
        sl = slice(hd * HEAD_SLAB, (hd + 1) * HEAD_SLAB)
        q_ref[hd] = (q[:, sl] * cos_q + qs[:, sl] * sin_q).astype(BF16)
        k_ref[hd] = (k[:, sl] * cos + ks[:, sl] * sin).astype(BF16)
    vt_ref[...] = _dot_nt(wuvt_ref[...], ckvn).astype(BF16)
    loc_ref[:, 0:2 * SGU_DIM] = proj[:, PROJ_A:PROJ_A + 2 * SGU_DIM].astype(BF16)
    loc_ref[:, 2 * SGU_DIM:] = proj[:, PROJ_W - FOURIER_DIM:].astype(BF16)
    pool_ref[...] = proj[:, PROJ_A + 2 * SGU_DIM:PROJ_A + 2 * SGU_DIM + POOL_DIM]


def _inproj(xx, mod_l, cos_t, sin_t, wl, dims):
    B, L, Lc = dims
    R = xx.shape[0]
    nlat = B * L // TM
    npb = L // TM

    def mod_idx(j):
        return (jnp.minimum(j // npb, B), 0, 0)

    def tab_idx(j):
        return (jnp.where(j < nlat, j % npb, npb), 0)

    return pl.pallas_call(
        _inproj_kernel,
        grid=(R // TM,),
        in_specs=[
            pl.BlockSpec((TM, D_MODEL), lambda j: (j, 0)),
            pl.BlockSpec((1, 1, 6 * D_MODEL), mod_idx),
            pl.BlockSpec((TM, HEAD_SLAB), tab_idx),
            pl.BlockSpec((TM, HEAD_SLAB), tab_idx),
            _resident((D_MODEL, PROJ_W)),
            _resident((1, Q_LORA)),
            _resident((1, KV_LORA)),
            _resident((Q_LORA, N_HEADS * HEAD_SLAB)),
            _resident((Q_LORA, N_HEADS * HEAD_SLAB)),
            _resident((PROJ_A - Q_LORA, N_HEADS * HEAD_SLAB)),
            _resident((PROJ_A - Q_LORA, N_HEADS * HEAD_SLAB)),
            _resident((ATTN_DIM, KV_LORA)),
        ],
        out_specs=[
            pl.BlockSpec((N_HEADS, TM, HEAD_SLAB), lambda j: (0, j, 0)),
            pl.BlockSpec((N_HEADS, TM, HEAD_SLAB), lambda j: (0, j, 0)),
            pl.BlockSpec((ATTN_DIM, TM), lambda j: (0, j)),
            pl.BlockSpec((TM, 2 * SGU_DIM + FOURIER_DIM), lambda j: (j, 0)),
            pl.BlockSpec((TM, POOL_DIM), lambda j: (j, 0)),
        ],
        out_shape=[
            jax.ShapeDtypeStruct((N_HEADS, R, HEAD_SLAB), BF16),
            jax.ShapeDtypeStruct((N_HEADS, R, HEAD_SLAB), BF16),
            jax.ShapeDtypeStruct((ATTN_DIM, R), BF16),
            jax.ShapeDtypeStruct((R, 2 * SGU_DIM + FOURIER_DIM), BF16),
            jax.ShapeDtypeStruct((R, POOL_DIM), F32),
        ],
        compiler_params=pltpu.CompilerParams(dimension_semantics=("arbitrary",), vmem_limit_bytes=VMEM_LIMIT),
        name="inproj",
    )(xx, mod_l, cos_t, sin_t, wl["w_in"], wl["q_norm"], wl["kv_norm"], wl["wq"], wl["wq_sw"],
      wl["wk"], wl["wk_sw"], wl["w_uvt"])


def _attn_kernel(*refs, with_latent):
    if with_latent:
        q_ref, kl_ref, kc_ref, vl_ref, vc_ref, o_ref, acc_ref = refs
    else:
        q_ref, kc_ref, vc_ref, o_ref, acc_ref = refs

    def head(hd, carry):
        qh = q_ref[hd]
        r0 = pl.multiple_of(hd * V_DIM, V_DIM)
        s_c = _dot_nt(kc_ref[hd], qh)
        m = jnp.max(s_c, axis=0, keepdims=True)
        if with_latent:
            s_l = _dot_nt(kl_ref[hd], qh)
            m = jnp.maximum(m, jnp.max(s_l, axis=0, keepdims=True))
        e_c = jnp.exp(s_c - m)
        den = jnp.sum(e_c, axis=0, keepdims=True)
        o = _dot(vc_ref[pl.ds(r0, V_DIM), :], e_c.astype(BF16))
        if with_latent:
            e_l = jnp.exp(s_l - m)
            den = den + jnp.sum(e_l, axis=0, keepdims=True)
            o = o + _dot(vl_ref[pl.ds(r0, V_DIM), :], e_l.astype(BF16))
        acc_ref[pl.ds(r0, V_DIM), :] = o / den
        return carry

    lax.fori_loop(0, N_HEADS, head, 0)
    o_ref[...] = acc_ref[...].T.astype(BF16)


def _attention_latent(q, k, vt, dims):
    B, L, Lc = dims
    R = q.shape[1]
    nq = L // TQ
    ctx_blk0 = B * L // Lc
    return pl.pallas_call(
        functools.partial(_attn_kernel, with_latent=True),
        grid=(B, nq),
        in_specs=[
            pl.BlockSpec((N_HEADS, TQ, HEAD_SLAB), lambda b, i: (0, b * nq + i, 0)),
            pl.BlockSpec((N_HEADS, L, HEAD_SLAB), lambda b, i: (0, b, 0)),
            pl.BlockSpec((N_HEADS, Lc, HEAD_SLAB), lambda b, i: (0, ctx_blk0 + b, 0)),
            pl.BlockSpec((ATTN_DIM, L), lambda b, i: (0, b)),
            pl.BlockSpec((ATTN_DIM, Lc), lambda b, i: (0, ctx_blk0 + b)),
        ],
        out_specs=pl.BlockSpec((TQ, ATTN_DIM), lambda b, i: (b * nq + i, 0)),
        out_shape=jax.ShapeDtypeStruct((R, ATTN_DIM), BF16),
        scratch_shapes=[pltpu.VMEM((ATTN_DIM, TQ), F32)],
        compiler_params=pltpu.CompilerParams(dimension_semantics=("arbitrary", "arbitrary"),
                                             vmem_limit_bytes=VMEM_LIMIT),
        name="attn_latent",
    )(q, k, k, vt, vt)


def _attention_context(attn, q, k, vt, dims):
    B, L, Lc = dims
    R = q.shape[1]
    ctx_blk0 = B * L // Lc

    def kern(attn_in_ref, *refs):
        del attn_in_ref
        _attn_kernel(*refs, with_latent=False)

    return pl.pallas_call(
        kern,
        grid=(B,),
        in_specs=[
            pl.BlockSpec(memory_space=pl.ANY),
            pl.BlockSpec((N_HEADS, Lc, HEAD_SLAB), lambda b: (0, ctx_blk0 + b, 0)),
            pl.BlockSpec((N_HEADS, Lc, HEAD_SLAB), lambda b: (0, ctx_blk0 + b, 0)),
            pl.BlockSpec((ATTN_DIM, Lc), lambda b: (0, ctx_blk0 + b)),
        ],
        out_specs=pl.BlockSpec((Lc, ATTN_DIM), lambda b: (ctx_blk0 + b, 0)),
        out_shape=jax.ShapeDtypeStruct((R, ATTN_DIM), BF16),
        scratch_shapes=[pltpu.VMEM((ATTN_DIM, Lc), F32)],
        input_output_aliases={0: 0},
        compiler_params=pltpu.CompilerParams(dimension_semantics=("arbitrary",)),
        name="attn_context",
    )(attn, q, k, vt)


def _local_kernel(loc_ref, pool_ref, dft_ref, cbd_ref, sbd_ref, wf_ref, sg_ref, sb_ref, ws_ref, bias_ref,
                  wp_ref, ps_ref, icnt_ref, o_ref, bufa, bufb, *, seq):
    lane_grp = lax.shift_right_logical(lax.broadcasted_iota(jnp.int32, (CHUNK, SGU_DIM), 1), SGU_GRP_SHIFT)
    sg = sg_ref[...]
    sb = sb_ref[...]

    def chunk(ci, carry):
        r0 = pl.multiple_of(ci * CHUNK, CHUNK)
        u = loc_ref[pl.ds(r0, CHUNK), 0:SGU_DIM].astype(F32)
        v = loc_ref[pl.ds(r0, CHUNK), SGU_DIM:2 * SGU_DIM].astype(F32)
        vn = _layer_norm(v, sg, sb)
        stacked = jnp.concatenate(
            [jnp.where(lane_grp == g, vn, 0.0).astype(BF16) for g in range(SGU_HEADS)], axis=0)
        mixed = _dot(ws_ref[...], stacked) + bias_ref[...]
        o_ref[pl.ds(r0, CHUNK), 0:SGU_DIM] = (u * mixed).astype(BF16)
        return carry

    lax.fori_loop(0, seq // CHUNK, chunk, 0)

    x = pool_ref[...]
    n = seq + POOL_PAD
    half = POOL_PAD // 2
    zpad = jnp.zeros((POOL_PAD, POOL_DIM), F32)
    bufa[0:POOL_PAD, :] = zpad
    bufa[seq + POOL_PAD:seq + 2 * POOL_PAD, :] = zpad
    bufa[POOL_PAD:seq + POOL_PAD, :] = x
    bufb[0:half, :] = zpad[0:half]
    bufb[n + half:n + POOL_PAD, :] = zpad[0:half]
    c2 = bufa[pl.ds(half - 1, n), :] + bufa[pl.ds(half, n), :]
    bufb[pl.ds(half, n), :] = c2
    c4 = bufb[pl.ds(half - 1, n), :] + bufb[pl.ds(half + 1, n), :]
    bufa[pl.ds(half, n), :] = c4
    c8 = bufa[pl.ds(half - 2, n), :] + bufa[pl.ds(half + 2, n), :]
    bufb[pl.ds(half, n), :] = c8
    c16 = bufb[pl.ds(half - 4, n), :] + bufb[pl.ds(half + 4, n), :]
    grp = lax.shift_right_logical(lax.broadcasted_iota(jnp.int32, (seq, POOL_DIM), 1), POOL_GRP_SHIFT)
    inner = slice(half, half + seq)
    wsum = jnp.where(grp == 0, c2[inner], jnp.where(grp == 1, c4[inner], jnp.where(grp == 2, c8[inner], c16[inner])))
    centred = (wsum * icnt_ref[...] - x).astype(BF16)
    o_ref[:, SGU_DIM:SGU_DIM + POOL_DIM] = (_dot(centred, wp_ref[...]) * ps_ref[...]).astype(BF16)

    f = loc_ref[:, 2 * SGU_DIM:]
    g = jnp.concatenate([_dot(f, cbd_ref[...]).astype(BF16), _dot(f, sbd_ref[...]).astype(BF16)], axis=0)
    spec = _dot(dft_ref[...], g)
    o_ref[:, SGU_DIM + POOL_DIM:] = _dot(spec.astype(BF16), wf_ref[...]).astype(BF16)


def _local_mixers(loc_in, pool_in, tabs, wl, seq, blk0, nseq, out=None):
    R = loc_in.shape[0]
    in_specs = [
        pl.BlockSpec((seq, 2 * SGU_DIM + FOURIER_DIM), lambda b: (blk0 + b, 0)),
        pl.BlockSpec((seq, POOL_DIM), lambda b: (blk0 + b, 0)),
        _resident((seq, 2 * seq)),
        _resident((FOURIER_DIM, FOURIER_DIM)),
        _resident((FOURIER_DIM, FOURIER_DIM)),
        _resident((FOURIER_DIM, FOURIER_DIM)),
        _resident((1, SGU_DIM)),
        _resident((1, SGU_DIM)),
        _resident((CHUNK, SGU_HEADS * CHUNK)),
        _resident((CHUNK, SGU_DIM)),
        _resident((POOL_DIM, POOL_DIM)),
        _resident((1, POOL_DIM)),
        _resident((seq, POOL_DIM)),
    ]
    args = [loc_in, pool_in, tabs["dft"], tabs["cbd"], tabs["sbd"], wl["w_f"], wl["sgu_g"], wl["sgu_b"],
            wl["ws_cat"], wl["sgu_bias"], wl["wp_bd"], wl["pool_scale"], tabs["icnt"]]
    kern = functools.partial(_local_kernel, seq=seq)
    aliases = {}
    if out is not None:
        in_specs = [pl.BlockSpec(memory_space=pl.ANY)] + in_specs
        args = [out] + args
        aliases = {0: 0}
        inner = kern

        def kern(out_in_ref, *refs):
            del out_in_ref
            inner(*refs)

    return pl.pallas_call(
        kern,
        grid=(nseq,),
        in_specs=in_specs,
        out_specs=pl.BlockSpec((seq, LOC_DIM), lambda b: (blk0 + b, 0)),
        out_shape=jax.ShapeDtypeStruct((R, LOC_DIM), BF16),
        scratch_shapes=[pltpu.VMEM((seq + 2 * POOL_PAD, POOL_DIM), F32),
                        pltpu.VMEM((seq + 2 * POOL_PAD, POOL_DIM), F32)],
        input_output_aliases=aliases,
        compiler_params=pltpu.CompilerParams(dimension_semantics=("arbitrary",), vmem_limit_bytes=VMEM_LIMIT),
        name="local_mixers_%d" % seq,
    )(*args)


def _tail_kernel(x_ref, attn_ref, loc_ref, mod_ref, woa_ref, wob_ref, l1g_ref, l1b_ref, w1_ref, w3_ref, w2_ref,
                 l2g_ref, l2b_ref, o_ref):
    g1 = mod_ref[0, :, 2 * D_MODEL:3 * D_MODEL]
    sh2 = mod_ref[0, :, 3 * D_MODEL:4 * D_MODEL]
    sc2 = mod_ref[0, :, 4 * D_MODEL:5 * D_MODEL]
    g2 = mod_ref[0, :, 5 * D_MODEL:6 * D_MODEL]
    y = _dot(attn_ref[...], woa_ref[...]) + _dot(loc_ref[...], wob_ref[...])
    x1 = _layer_norm(DEEPNORM_ALPHA * x_ref[...] + g1 * y, l1g_ref[...], l1b_ref[...])
    h = (x1 * (1.0 + sc2) + sh2).astype(BF16)
    a = _dot(h, w1_ref[...])
    b = _dot(h, w3_ref[...])
    hid = (a * (1.0 / (1.0 + jnp.exp(-a))) * b).astype(BF16)
    ffn = _dot(hid, w2_ref[...])
    o_ref[...] = _layer_norm(DEEPNORM_ALPHA * x1 + g2 * ffn, l2g_ref[...], l2b_ref[...])


def _tail(xx, attn, loc, mod_l, wl, dims, rows):
    B, L, Lc = dims
    npb = L // TM

    def mod_idx(j):
        return (jnp.minimum(j // npb, B), 0, 0)

    return pl.pallas_call(
        _tail_kernel,
        grid=(rows // TM,),
        in_specs=[
            pl.BlockSpec((TM, D_MODEL), lambda j: (j, 0)),
            pl.BlockSpec((TM, ATTN_DIM), lambda j: (j, 0)),
            pl.BlockSpec((TM, LOC_DIM), lambda j: (j, 0)),
            pl.BlockSpec((1, 1, 6 * D_MODEL), mod_idx),
            _resident((ATTN_DIM, D_MODEL)),
            _resident((LOC_DIM, D_MODEL)),
            _resident((1, D_MODEL)),
            _resident((1, D_MODEL)),
            _resident((D_MODEL, D_FF)),
            _resident((D_MODEL, D_FF)),
            _resident((D_FF, D_MODEL)),
            _resident((1, D_MODEL)),
            _resident((1, D_MODEL)),
        ],
        out_specs=pl.BlockSpec((TM, D_MODEL), lambda j: (j, 0)),
        out_shape=jax.ShapeDtypeStruct((rows, D_MODEL), F32),
        compiler_params=pltpu.CompilerParams(dimension_semantics=("arbitrary",), vmem_limit_bytes=VMEM_LIMIT),
        name="tail",
    )(xx, attn, loc, mod_l, wl["w_out_a"], wl["w_out_b"], wl["ln1_g"], wl["ln1_b"], wl["w1"], wl["w3"], wl["w2"],
      wl["ln2_g"], wl["ln2_b"])


def _rope_tables(L):
    pos = jnp.arange(L)
    row = (pos // GRID_W).astype(F32)
    col = (pos % GRID_W).astype(F32)
    n_freq = QK_ROPE // 4
    inv_freq = ROPE_BASE ** (-jnp.arange(n_freq, dtype=F32) / n_freq)
    ar = row[:, None] * inv_freq
    ac = col[:, None] * inv_freq
    ones = jnp.ones((L, QK_NOPE), F32)
    zeros = jnp.zeros((L, QK_NOPE), F32)
    pad1 = jnp.ones((L, HEAD_SLAB - QK_NOPE - QK_ROPE), F32)
    pad0 = jnp.zeros((L, HEAD_SLAB - QK_NOPE - QK_ROPE), F32)
    cos = jnp.concatenate([ones, jnp.cos(ar), jnp.cos(ar), jnp.cos(ac), jnp.cos(ac), pad1], axis=1)
    sin = jnp.concatenate([zeros, -jnp.sin(ar), jnp.sin(ar), -jnp.sin(ac), jnp.sin(ac), pad0], axis=1)
    cos = jnp.concatenate([cos, jnp.ones((TM, HEAD_SLAB), F32)], axis=0)
    sin = jnp.concatenate([sin, jnp.zeros((TM, HEAD_SLAB), F32)], axis=0)
    return cos, sin


def _dft_tables(seq):
    idx = np.arange(seq, dtype=np.int64)
    ang = 2.0 * np.pi * ((idx[:, None] * idx[None, :]) % seq).astype(np.float64) / seq
    dft = np.concatenate([np.cos(ang), -np.sin(ang)], axis=1)
    ch = FOURIER_DIM // FOURIER_HEADS
    cidx = np.arange(ch, dtype=np.int64)
    cang = 2.0 * np.pi * ((cidx[:, None] * cidx[None, :]) % ch).astype(np.float64) / ch
    norm = 1.0 / np.sqrt(float(seq) * ch)
    eye = np.eye(FOURIER_HEADS)
    cbd = np.kron(eye, np.cos(cang)) * norm
    sbd = np.kron(eye, np.sin(cang)) * norm
    as_bf16 = lambda a: jnp.asarray(a.astype(np.float32)).astype(BF16)
    return as_bf16(dft), as_bf16(cbd), as_bf16(sbd)


def _pool_inv_counts(seq):
    t = np.arange(seq)
    cols = []
    for w in POOL_WINDOWS:
        lo = np.clip(t - w // 2, 0, seq)
        hi = np.clip(t + w // 2, 0, seq)
        cols.append(np.repeat((1.0 / (hi - lo).astype(np.float64))[:, None], POOL_GDIM, axis=1))
    return jnp.asarray(np.concatenate(cols, axis=1), dtype=F32)


def _seq_tables(seq):
    dft, cbd, sbd = _dft_tables(seq)
    return {"dft": dft, "cbd": cbd, "sbd": sbd, "icnt": _pool_inv_counts(seq)}


def _prep_weights(w_in, q_norm, w_uq, kv_norm, w_uk, w_uv, sgu_ln_g, sgu_ln_b, w_spatial, b_spatial, w_pool,
                  pool_scale, w_fourier, w_out, ln1_g, ln1_b, w_ffn1, w_ffn3, w_ffn2, ln2_g, ln2_b):
    rope_dim = QK_NOPE + QK_ROPE
    slab_pad = HEAD_SLAB - rope_dim
    w_in_p = jnp.concatenate(
        [w_in[..., :OFF_SGU], jnp.zeros((DEPTH, D_MODEL, PROJ_A - OFF_SGU), F32), w_in[..., OFF_SGU:]], axis=-1)
    e = QK_ROPE // 4
    swap = np.concatenate([np.arange(e, 2 * e), np.arange(0, e), np.arange(3 * e, 4 * e), np.arange(2 * e, 3 * e)])
    wq4 = w_uq.reshape(DEPTH, Q_LORA, N_HEADS, rope_dim)
    wq = jnp.pad(wq4, ((0, 0), (0, 0), (0, 0), (0, slab_pad)))
    wq_sw = jnp.concatenate([jnp.zeros((DEPTH, Q_LORA, N_HEADS, QK_NOPE), F32), wq4[..., QK_NOPE:][..., swap],
                             jnp.zeros((DEPTH, Q_LORA, N_HEADS, slab_pad), F32)], axis=-1)
    krows = PROJ_A - Q_LORA
    wk_nope = jnp.pad(w_uk.reshape(DEPTH, KV_LORA, N_HEADS, QK_NOPE), ((0, 0), (0, 0), (0, 0), (0, HEAD_SLAB - QK_NOPE)))
    place = np.zeros((QK_ROPE, HEAD_SLAB), np.float32)
    place[np.arange(QK_ROPE), QK_NOPE + np.arange(QK_ROPE)] = 1.0
    place_sw = np.zeros((QK_ROPE, HEAD_SLAB), np.float32)
    place_sw[swap, QK_NOPE + np.arange(QK_ROPE)] = 1.0
    place_all = jnp.broadcast_to(jnp.asarray(np.tile(place, (1, N_HEADS)))[None], (DEPTH, QK_ROPE, N_HEADS * HEAD_SLAB))
    place_sw_all = jnp.broadcast_to(jnp.asarray(np.tile(place_sw, (1, N_HEADS)))[None],
                                    (DEPTH, QK_ROPE, N_HEADS * HEAD_SLAB))
    ztail = jnp.zeros((DEPTH, krows - KV_LORA - QK_ROPE, N_HEADS * HEAD_SLAB), F32)
    wk = jnp.concatenate([wk_nope.reshape(DEPTH, KV_LORA, N_HEADS * HEAD_SLAB), place_all, ztail], axis=1)
    wk_sw = jnp.concatenate([jnp.zeros((DEPTH, KV_LORA, N_HEADS * HEAD_SLAB), F32), place_sw_all, ztail], axis=1)
    ws_cat = jnp.transpose(w_spatial, (0, 2, 1, 3)).reshape(DEPTH, CHUNK, SGU_HEADS * CHUNK)
    sgu_bias = jnp.repeat(jnp.transpose(b_spatial, (0, 2, 1)), SGU_DIM // SGU_HEADS, axis=2)
    wp_bd = jnp.zeros((DEPTH, POOL_DIM, POOL_DIM), F32)
    for gi in range(len(POOL_WINDOWS)):
        sl = slice(gi * POOL_GDIM, (gi + 1) * POOL_GDIM)
        wp_bd = wp_bd.at[:, sl, sl].set(w_pool[:, gi])
    bf = lambda a: a.astype(BF16)
    row = lambda a: a.reshape(DEPTH, 1, a.shape[-1])
    return {
        "w_in": bf(w_in_p), "q_norm": row(q_norm), "kv_norm": row(kv_norm),
        "wq": bf(wq.reshape(DEPTH, Q_LORA, N_HEADS * HEAD_SLAB)),
        "wq_sw": bf(wq_sw.reshape(DEPTH, Q_LORA, N_HEADS * HEAD_SLAB)),
        "wk": bf(wk), "wk_sw": bf(wk_sw), "w_uvt": bf(jnp.transpose(w_uv, (0, 2, 1))),
        "sgu_g": row(sgu_ln_g), "sgu_b": row(sgu_ln_b), "ws_cat": bf(ws_cat), "sgu_bias": sgu_bias,
        "wp_bd": bf(wp_bd), "pool_scale": row(pool_scale), "w_f": bf(w_fourier),
        "w_out_a": bf(w_out[:, :ATTN_DIM]), "w_out_b": bf(w_out[:, ATTN_DIM:]),
        "ln1_g": row(ln1_g), "ln1_b": row(ln1_b), "w1": bf(w_ffn1), "w3": bf(w_ffn3), "w2": bf(w_ffn2),
        "ln2_g": row(ln2_g), "ln2_b": row(ln2_b),
    }


def kernel(x, c, ctx, c_ctx, w_mod, b_mod, w_in, q_norm, w_uq, kv_norm, w_uk, w_uv, sgu_ln_g, sgu_ln_b, w_spatial,
           b_spatial, w_pool, pool_scale, w_fourier, w_out, ln1_g, ln1_b, w_ffn1, w_ffn3, w_ffn2, ln2_g, ln2_b):
    B, L, D = x.shape
    Lc = ctx.shape[1]
    assert D == D_MODEL and L % TM == 0 and L % TQ == 0 and L % GRID_W == 0 and L % Lc == 0
    assert TM % Lc == 0 and (B * Lc) % TM == 0 and Lc % CHUNK == 0 and B < MOD_ROWS
    dims = (B, L, Lc)
    r_lat = B * L
    xx = jnp.concatenate([x.reshape(r_lat, D), ctx.reshape(B * Lc, D)], axis=0)

    cc = jnp.concatenate([c, c_ctx[None, :], jnp.zeros((MOD_ROWS - B - 1, D), F32)], axis=0)
    mod_all = _modulation(cc, w_mod, b_mod)

    wts = _prep_weights(w_in, q_norm, w_uq, kv_norm, w_uk, w_uv, sgu_ln_g, sgu_ln_b, w_spatial, b_spatial, w_pool,
                        pool_scale, w_fourier, w_out, ln1_g, ln1_b, w_ffn1, w_ffn3, w_ffn2, ln2_g, ln2_b)
    cos_t, sin_t = _rope_tables(L)
    tabs_lat = _seq_tables(L)
    tabs_ctx = _seq_tables(Lc)

    for l in range(DEPTH):
        last = l == DEPTH - 1
        wl = {k: v[l] for k, v in wts.items()}
        mod_l = mod_all[l].reshape(MOD_ROWS, 1, 6 * D)
        q, k, vt, loc_in, pool_in = _inproj(xx, mod_l, cos_t, sin_t, wl, dims)
        attn = _attention_latent(q, k, vt, dims)
        loc = _local_mixers(loc_in, pool_in, tabs_lat, wl, L, 0, B)
        if not last:
            attn = _attention_context(attn, q, k, vt, dims)
            loc = _local_mixers(loc_in, pool_in, tabs_ctx, wl, Lc, r_lat // Lc, B, out=loc)
        xx = _tail(xx, attn, loc, mod_l, wl, dims, r_lat if last else xx.shape[0])
    return xx.reshape(B, L, D)
```

```python
import functools
import math

import numpy as np
import jax
import jax.numpy as jnp
from jax import lax
from jax.experimental import pallas as pl
from jax.experimental.pallas import tpu as pltpu

F32 = jnp.float32
BF16 = jnp.bfloat16

D_MODEL = 1024
DEPTH = 4
GRID_W = 64
N_HEADS = 8
QK_NOPE = 64
QK_ROPE = 32
V_DIM = 64
Q_LORA = 256
KV_LORA = 128
ROPE_BASE = 10000.0
SGU_DIM = 256
SGU_HEADS = 4
CHUNK = 128
POOL_DIM = 256
POOL_WINDOWS = (2, 4, 8, 16)
POOL_GDIM = POOL_DIM // len(POOL_WINDOWS)
FOURIER_DIM = 256
FOURIER_HEADS = 4
ATTN_DIM = N_HEADS * V_DIM
LOC_DIM = SGU_DIM + POOL_DIM + FOURIER_DIM
LOC_IN = 2 * SGU_DIM + FOURIER_DIM
OFF_SGU = Q_LORA + KV_LORA + QK_ROPE
IN_DIM = OFF_SGU + 2 * SGU_DIM + POOL_DIM + FOURIER_DIM
D_FF = ((8 * D_MODEL + 3 * 256 - 1) // (3 * 256)) * 256
LN_EPS = 1e-6
DEEPNORM_ALPHA = (2.0 * DEPTH) ** 0.25
QK_SCALE_LOG2 = (QK_NOPE + QK_ROPE) ** -0.5 * math.log2(math.e)
SGU_GRP_SHIFT = (SGU_DIM // SGU_HEADS).bit_length() - 1
POOL_GRP_SHIFT = POOL_GDIM.bit_length() - 1

LANE = 128
BF16_SUBLANES = 16
HEAD_SLAB = LANE
V_SLAB = V_DIM + BF16_SUBLANES
VT_ROWS = N_HEADS * V_SLAB
PROJ_A = 512
PROJ_W = PROJ_A + 2 * SGU_DIM + POOL_DIM + FOURIER_DIM
POOL_PAD = 16
VMEM_LIMIT = 56 * 1024 * 1024

TM = 512
TQ = 512
MOD_NB = 1536
MOD_ROWS = 16


def _layer_block(shape, l):
    n = len(shape)
    return pl.BlockSpec((None,) + tuple(shape), lambda *_: (l,) + (0,) * n, pipeline_mode=pl.Buffered(1))


def _resident(shape):
    n = len(shape)
    return pl.BlockSpec(shape, lambda *_: (0,) * n, pipeline_mode=pl.Buffered(1))


def _dot(a, b):
    return jnp.dot(a, b, preferred_element_type=F32)


def _dot_nt(a, b):
    return lax.dot_general(a, b, (((1,), (1,)), ((), ())), preferred_element_type=F32)


def _layer_norm(x, g, b):
    mu = jnp.mean(x, axis=-1, keepdims=True)
    xc = x - mu
    var = jnp.mean(xc * xc, axis=-1, keepdims=True)
    return xc * lax.rsqrt(var + LN_EPS) * g + b


def _rms_norm(x, g):
    return x * lax.rsqrt(jnp.mean(x * x, axis=-1, keepdims=True) + LN_EPS) * g


def _mod_kernel(c_ref, w_ref, b_ref, o_ref):
    c = c_ref[...]
    s = c * (1.0 / (1.0 + jnp.exp(-c)))
    o_ref[0] = _dot(s.astype(BF16), w_ref[0].astype(BF16)) + b_ref[0]


def _modulation(cc, w_mod, b_mod):
    d6 = w_mod.shape[-1]
    return pl.pallas_call(
        _mod_kernel,
        grid=(DEPTH, d6 // MOD_NB),
        in_specs=[
            pl.BlockSpec((MOD_ROWS, D_MODEL), lambda l, n: (0, 0)),
            pl.BlockSpec((1, D_MODEL, MOD_NB), lambda l, n: (l, 0, n)),
            pl.BlockSpec((1, 1, MOD_NB), lambda l, n: (l, 0, n)),
        ],
        out_specs=pl.BlockSpec((1, MOD_ROWS, MOD_NB), lambda l, n: (l, 0, n)),
        out_shape=jax.ShapeDtypeStruct((DEPTH, MOD_ROWS, d6), F32),
        compiler_params=pltpu.CompilerParams(dimension_semantics=("arbitrary", "arbitrary")),
        name="modulation",
    )(cc, w_mod, b_mod.reshape(DEPTH, 1, d6))


def _mod_spec(l, npb, B):
    return pl.BlockSpec((None, 1, 1, 6 * D_MODEL), lambda j: (l, jnp.minimum(j // npb, B), 0, 0))


def _token_rows(x_refs, nlat):
    if len(x_refs) == 1:
        return x_refs[0][...]
    return jnp.where(pl.program_id(0) < nlat, x_refs[0][...], x_refs[1][...])


def _token_specs(xs, nlat):
    if len(xs) == 1:
        return [pl.BlockSpec((TM, D_MODEL), lambda j: (j, 0))]
    return [pl.BlockSpec((TM, D_MODEL), lambda j: (jnp.minimum(j, nlat - 1), 0)),
            pl.BlockSpec((TM, D_MODEL), lambda j: (jnp.maximum(j - nlat, 0), 0))]


def _inproj_kernel(*refs, nx, nlat):
    (mod_ref, cos_ref, sin_ref, win_ref, qn_ref, kvn_ref, wq_ref, wqs_ref,
     wk_ref, wks_ref, wuvt_ref, vones_ref, q_ref, k_ref, vt_ref, loc_ref, pool_ref) = refs[nx:]
    x = _token_rows(refs[:nx], nlat)
    sh1 = mod_ref[0, :, 0:D_MODEL]
    sc1 = mod_ref[0, :, D_MODEL:2 * D_MODEL]
    h = (x * (1.0 + sc1) + sh1).astype(BF16)
    proj = _dot(h, win_ref[...])
    cqn = _rms_norm(proj[:, 0:Q_LORA], qn_ref[...]).astype(BF16)
    ckvn = _rms_norm(proj[:, Q_LORA:Q_LORA + KV_LORA], kvn_ref[...]).astype(BF16)
    xk = jnp.concatenate([ckvn, proj[:, Q_LORA + KV_LORA:PROJ_A].astype(BF16)], axis=1)
    q = _dot(cqn, wq_ref[...])
    qs = _dot(cqn, wqs_ref[...])
    k = _dot(xk, wk_ref[...])
    ks = _dot(xk, wks_ref[...])
    cos = cos_ref[...]
    sin = sin_ref[...]
    cos_q = cos * QK_SCALE_LOG2
    sin_q = sin * QK_SCALE_LOG2
    for hd in range(N_HEADS):
        sl = slice(hd * HEAD_SLAB, (hd + 1) * HEAD_SLAB)
        q_ref[hd] = (q[:, sl] * cos_q + qs[:, sl] * sin_q).astype(BF16)
        k_ref[hd] = (k[:, sl] * cos + ks[:, sl] * sin).astype(BF16)
    vt_ref[...] = (_dot_nt(wuvt_ref[...], ckvn) + vones_ref[...]).astype(BF16)
    loc_ref[:, 0:2 * SGU_DIM] = proj[:, PROJ_A:PROJ_A + 2 * SGU_DIM].astype(BF16)
    loc_ref[:, 2 * SGU_DIM:] = proj[:, PROJ_W - FOURIER_DIM:].astype(BF16)
    pool_ref[...] = proj[:, PROJ_A + 2 * SGU_DIM:PROJ_A + 2 * SGU_DIM + POOL_DIM]


def _inproj(l, xs, mod_all, cos_t, sin_t, wts, dims):
    B, L, Lc = dims
    R = B * (L + Lc)
    nlat = B * L // TM
    npb = L // TM

    def tab_idx(j):
        return (jnp.where(j < nlat, j % npb, npb), 0)

    return pl.pallas_call(
        functools.partial(_inproj_kernel, nx=len(xs), nlat=nlat),
        grid=(R // TM,),
        in_specs=_token_specs(xs, nlat) + [
            _mod_spec(l, npb, B),
            pl.BlockSpec((TM, HEAD_SLAB), tab_idx),
            pl.BlockSpec((TM, HEAD_SLAB), tab_idx),
            _layer_block((D_MODEL, PROJ_W), l),
            _layer_block((1, Q_LORA), l),
            _layer_block((1, KV_LORA), l),
            _layer_block((Q_LORA, N_HEADS * HEAD_SLAB), l),
            _layer_block((Q_LORA, N_HEADS * HEAD_SLAB), l),
            _layer_block((PROJ_A - Q_LORA, N_HEADS * HEAD_SLAB), l),
            _layer_block((PROJ_A - Q_LORA, N_HEADS * HEAD_SLAB), l),
            _layer_block((VT_ROWS, KV_LORA), l),
            _resident((VT_ROWS, 1)),
        ],
        out_specs=[
            pl.BlockSpec((N_HEADS, TM, HEAD_SLAB), lambda j: (0, j, 0)),
            pl.BlockSpec((N_HEADS, TM, HEAD_SLAB), lambda j: (0, j, 0)),
            pl.BlockSpec((VT_ROWS, TM), lambda j: (0, j)),
            pl.BlockSpec((TM, LOC_IN), lambda j: (j, 0)),
            pl.BlockSpec((TM, POOL_DIM), lambda j: (j, 0)),
        ],
        out_shape=[
            jax.ShapeDtypeStruct((N_HEADS, R, HEAD_SLAB), BF16),
            jax.ShapeDtypeStruct((N_HEADS, R, HEAD_SLAB), BF16),
            jax.ShapeDtypeStruct((VT_ROWS, R), BF16),
            jax.ShapeDtypeStruct((R, LOC_IN), BF16),
            jax.ShapeDtypeStruct((R, POOL_DIM), F32),
        ],
        compiler_params=pltpu.CompilerParams(dimension_semantics=("arbitrary",), vmem_limit_bytes=VMEM_LIMIT),
        name="inproj",
    )(*xs, mod_all, cos_t, sin_t, wts["w_in"], wts["q_norm"], wts["kv_norm"], wts["wq"], wts["wq_sw"],
      wts["wk"], wts["wk_sw"], wts["w_uvt"], wts["v_ones"])


def _attn_kernel(*refs, with_latent):
    if with_latent:
        q_ref, kl_ref, kc_ref, vl_ref, vc_ref, o_ref, acc_ref, sla_ref, sca_ref, slb_ref, scb_ref = refs
        bufs = ((sla_ref, sca_ref), (slb_ref, scb_ref))
    else:
        q_ref, kc_ref, vc_ref, o_ref, acc_ref, sca_ref, scb_ref = refs
        bufs = ((None, sca_ref), (None, scb_ref))

    def scores(hd, buf):
        sl_ref, sc_ref = buf
        qh = q_ref[hd]
        sc_ref[...] = _dot_nt(kc_ref[hd], qh)
        if with_latent:
            sl_ref[...] = _dot_nt(kl_ref[hd], qh)

    def soft_pv(hd, buf):
        sl_ref, sc_ref = buf
        r0 = pl.multiple_of(hd * V_SLAB, BF16_SUBLANES)
        m = jnp.max(sc_ref[...], axis=0, keepdims=True)
        if with_latent:
            m = jnp.maximum(m, jnp.max(sl_ref[...], axis=0, keepdims=True))
        o = _dot(vc_ref[pl.ds(r0, V_SLAB), :], jnp.exp2(sc_ref[...] - m).astype(BF16))
        if with_latent:
            o = o + _dot(vl_ref[pl.ds(r0, V_SLAB), :], jnp.exp2(sl_ref[...] - m).astype(BF16))
        a0 = pl.multiple_of(hd * V_DIM, V_DIM)
        acc_ref[pl.ds(a0, V_DIM), :] = o[0:V_DIM] / o[V_DIM:V_DIM + 1]

    scores(0, bufs[0])

    def pair(i, carry):
        h0 = 2 * i
        scores(h0 + 1, bufs[1])
        soft_pv(h0, bufs[0])
        scores(h0 + 2, bufs[0])
        soft_pv(h0 + 1, bufs[1])
        return carry

    lax.fori_loop(0, N_HEADS // 2 - 1, pair, 0)
    scores(N_HEADS - 1, bufs[1])
    soft_pv(N_HEADS - 2, bufs[0])
    soft_pv(N_HEADS - 1, bufs[1])
    o_ref[...] = acc_ref[...].T.astype(BF16)


def _attention_latent(q, k, vt, dims):
    B, L, Lc = dims
    nq = L // TQ
    ctx_blk0 = B * L // Lc
    return pl.pallas_call(
        functools.partial(_attn_kernel, with_latent=True),
        grid=(B, nq),
        in_specs=[
            pl.BlockSpec((N_HEADS, TQ, HEAD_SLAB), lambda b, i: (0, b * nq + i, 0)),
            pl.BlockSpec((N_HEADS, L, HEAD_SLAB), lambda b, i: (0, b, 0)),
            pl.BlockSpec((N_HEADS, Lc, HEAD_SLAB), lambda b, i: (0, ctx_blk0 + b, 0)),
            pl.BlockSpec((VT_ROWS, L), lambda b, i: (0, b)),
            pl.BlockSpec((VT_ROWS, Lc), lambda b, i: (0, ctx_blk0 + b)),
        ],
        out_specs=pl.BlockSpec((TQ, ATTN_DIM), lambda b, i: (b * nq + i, 0)),
        out_shape=jax.ShapeDtypeStruct((B * L, ATTN_DIM), BF16),
        scratch_shapes=[pltpu.VMEM((ATTN_DIM, TQ), F32),
                        pltpu.VMEM((L, TQ), F32), pltpu.VMEM((Lc, TQ), F32),
                        pltpu.VMEM((L, TQ), F32), pltpu.VMEM((Lc, TQ), F32)],
        compiler_params=pltpu.CompilerParams(dimension_semantics=("arbitrary", "arbitrary"),
                                             vmem_limit_bytes=VMEM_LIMIT),
        name="attn_latent",
    )(q, k, k, vt, vt)


def _attention_context(q, k, vt, dims):
    B, L, Lc = dims
    ctx_blk0 = B * L // Lc
    return pl.pallas_call(
        functools.partial(_attn_kernel, with_latent=False),
        grid=(B,),
        in_specs=[
            pl.BlockSpec((N_HEADS, Lc, HEAD_SLAB), lambda b: (0, ctx_blk0 + b, 0)),
            pl.BlockSpec((N_HEADS, Lc, HEAD_SLAB), lambda b: (0, ctx_blk0 + b, 0)),
            pl.BlockSpec((VT_ROWS, Lc), lambda b: (0, ctx_blk0 + b)),
        ],
        out_specs=pl.BlockSpec((Lc, ATTN_DIM), lambda b: (b, 0)),
        out_shape=jax.ShapeDtypeStruct((B * Lc, ATTN_DIM), BF16),
        scratch_shapes=[pltpu.VMEM((ATTN_DIM, Lc), F32), pltpu.VMEM((Lc, Lc), F32), pltpu.VMEM((Lc, Lc), F32)],
        compiler_params=pltpu.CompilerParams(dimension_semantics=("arbitrary",)),
        name="attn_context",
    )(q, k, vt)


def _local_kernel(loc_ref, pool_ref, dft_ref, cbd_ref, sbd_ref, wf_ref, sg_ref, sb_ref, ws_ref, bias_ref,
                  wp_ref, ps_ref, icnt_ref, o_ref, bufa, bufb, *, seq):
    lane_grp = lax.shift_right_logical(lax.broadcasted_iota(jnp.int32, (CHUNK, SGU_DIM), 1), SGU_GRP_SHIFT)
    sg = sg_ref[...]
    sb = sb_ref[...]

    def chunk(ci, carry):
        r0 = pl.multiple_of(ci * CHUNK, CHUNK)
        u = loc_ref[pl.ds(r0, CHUNK), 0:SGU_DIM].astype(F32)
        v = loc_ref[pl.ds(r0, CHUNK), SGU_DIM:2 * SGU_DIM].astype(F32)
        vn = _layer_norm(v, sg, sb)
        stacked = jnp.concatenate(
            [jnp.where(lane_grp == g, vn, 0.0).astype(BF16) for g in range(SGU_HEADS)], axis=0)
        mixed = _dot(ws_ref[...], stacked) + bias_ref[...]
        o_ref[pl.ds(r0, CHUNK), 0:SGU_DIM] = (u * mixed).astype(BF16)
        return carry

    lax.fori_loop(0, seq // CHUNK, chunk, 0)

    x = pool_ref[...]
    n = seq + POOL_PAD
    half = POOL_PAD // 2
    zpad = jnp.zeros((POOL_PAD, POOL_DIM), F32)
    bufa[0:POOL_PAD, :] = zpad
    bufa[seq + POOL_PAD:seq + 2 * POOL_PAD, :] = zpad
    bufa[POOL_PAD:seq + POOL_PAD, :] = x
    bufb[0:half, :] = zpad[0:half]
    bufb[n + half:n + POOL_PAD, :] = zpad[0:half]
    c2 = bufa[pl.ds(half - 1, n), :] + bufa[pl.ds(half, n), :]
    bufb[pl.ds(half, n), :] = c2
    c4 = bufb[pl.ds(half - 1, n), :] + bufb[pl.ds(half + 1, n), :]
    bufa[pl.ds(half, n), :] = c4
    c8 = bufa[pl.ds(half - 2, n), :] + bufa[pl.ds(half + 2, n), :]
    bufb[pl.ds(half, n), :] = c8
    c16 = bufb[pl.ds(half - 4, n), :] + bufb[pl.ds(half + 4, n), :]
    grp = lax.shift_right_logical(lax.broadcasted_iota(jnp.int32, (seq, POOL_DIM), 1), POOL_GRP_SHIFT)
    inner = slice(half, half + seq)
    wsum = jnp.where(grp == 0, c2[inner], jnp.where(grp == 1, c4[inner], jnp.where(grp == 2, c8[inner], c16[inner])))
    centred = (wsum * icnt_ref[...] - x).astype(BF16)
    o_ref[:, SGU_DIM:SGU_DIM + POOL_DIM] = (_dot(centred, wp_ref[...]) * ps_ref[...]).astype(BF16)

    f = loc_ref[:, 2 * SGU_DIM:]
    g = jnp.concatenate([_dot(f, cbd_ref[...]).astype(BF16), _dot(f, sbd_ref[...]).astype(BF16)], axis=0)
    spec = _dot(dft_ref[...], g)
    o_ref[:, SGU_DIM + POOL_DIM:] = _dot(spec.astype(BF16), wf_ref[...]).astype(BF16)


def _local_mixers(l, loc_in, pool_in, tabs, wts, seq, blk0, nseq):
    return pl.pallas_call(
        functools.partial(_local_kernel, seq=seq),
        grid=(nseq,),
        in_specs=[
            pl.BlockSpec((seq, LOC_IN), lambda b: (blk0 + b, 0)),
            pl.BlockSpec((seq, POOL_DIM), lambda b: (blk0 + b, 0)),
            _resident((seq, 2 * seq)),
            _resident((FOURIER_DIM, FOURIER_DIM)),
            _resident((FOURIER_DIM, FOURIER_DIM)),
            _layer_block((FOURIER_DIM, FOURIER_DIM), l),
            _layer_block((1, SGU_DIM), l),
            _layer_block((1, SGU_DIM), l),
            _layer_block((CHUNK, SGU_HEADS * CHUNK), l),
            _layer_block((CHUNK, SGU_DIM), l),
            _layer_block((POOL_DIM, POOL_DIM), l),
            _layer_block((1, POOL_DIM), l),
            _resident((seq, POOL_DIM)),
        ],
        out_specs=pl.BlockSpec((seq, LOC_DIM), lambda b: (b, 0)),
        out_shape=jax.ShapeDtypeStruct((nseq * seq, LOC_DIM), BF16),
        scratch_shapes=[pltpu.VMEM((seq + 2 * POOL_PAD, POOL_DIM), F32),
                        pltpu.VMEM((seq + 2 * POOL_PAD, POOL_DIM), F32)],
        compiler_params=pltpu.CompilerParams(dimension_semantics=("arbitrary",), vmem_limit_bytes=VMEM_LIMIT),
        name="local_mixers_%d" % seq,
    )(loc_in, pool_in, tabs["dft"], tabs["cbd"], tabs["sbd"], wts["w_f"], wts["sgu_g"], wts["sgu_b"],
      wts["ws_cat"], wts["sgu_bias"], wts["wp_bd"], wts["pool_scale"], tabs["icnt"])


def _tail_kernel(*refs, nx, nlat, with_ctx):
    x = _token_rows(refs[:nx], nlat)
    refs = refs[nx:]
    if not with_ctx:
        (attn_ref, loc_ref, mod_ref, woa_ref, wob_ref, l1g_ref, l1b_ref, w1_ref, w3_ref, w2_ref,
         l2g_ref, l2b_ref, o_ref) = refs
        attn = attn_ref[...]
        loc = loc_ref[...]
    else:
        (attn_ref, attnc_ref, loc_ref, locc_ref, mod_ref, woa_ref, wob_ref, l1g_ref, l1b_ref, w1_ref, w3_ref,
         w2_ref, l2g_ref, l2b_ref, o_ref) = refs
        is_lat = pl.program_id(0) < nlat
        attn = jnp.where(is_lat, attn_ref[...], attnc_ref[...])
        loc = jnp.where(is_lat, loc_ref[...], locc_ref[...])
    g1 = mod_ref[0, :, 2 * D_MODEL:3 * D_MODEL]
    sh2 = mod_ref[0, :, 3 * D_MODEL:4 * D_MODEL]
    sc2 = mod_ref[0, :, 4 * D_MODEL:5 * D_MODEL]
    g2 = mod_ref[0, :, 5 * D_MODEL:6 * D_MODEL]
    y = _dot(attn, woa_ref[...]) + _dot(loc, wob_ref[...])
    x1 = _layer_norm(DEEPNORM_ALPHA * x + g1 * y, l1g_ref[...], l1b_ref[...])
    h = (x1 * (1.0 + sc2) + sh2).astype(BF16)
    a = _dot(h, w1_ref[...])
    b = _dot(h, w3_ref[...])
    hid = (a * (1.0 / (1.0 + jnp.exp(-a))) * b).astype(BF16)
    ffn = _dot(hid, w2_ref[...])
    o_ref[...] = _layer_norm(DEEPNORM_ALPHA * x1 + g2 * ffn, l2g_ref[...], l2b_ref[...])


def _tail(l, xs, attn, loc, ctx_parts, mod_all, wts, dims):
    B, L, Lc = dims
    npb = L // TM
    nlat = B * L // TM
    rows = B * L if ctx_parts is None else B * (L + Lc)
    lat_idx = lambda j: (jnp.minimum(j, nlat - 1), 0)
    ctx_idx = lambda j: (jnp.maximum(j - nlat, 0), 0)
    act_specs = [pl.BlockSpec((TM, ATTN_DIM), lat_idx), pl.BlockSpec((TM, LOC_DIM), lat_idx)]
    acts = [attn, loc]
    if ctx_parts is not None:
        act_specs = [act_specs[0], pl.BlockSpec((TM, ATTN_DIM), ctx_idx), act_specs[1],
                     pl.BlockSpec((TM, LOC_DIM), ctx_idx)]
        acts = [attn, ctx_parts[0], loc, ctx_parts[1]]
    return pl.pallas_call(
        functools.partial(_tail_kernel, nx=len(xs), nlat=nlat, with_ctx=ctx_parts is not None),
        grid=(rows // TM,),
        in_specs=_token_specs(xs, nlat) + act_specs + [
            _mod_spec(l, npb, B),
            _layer_block((ATTN_DIM, D_MODEL), l),
            _layer_block((LOC_DIM, D_MODEL), l),
            _layer_block((1, D_MODEL), l),
            _layer_block((1, D_MODEL), l),
            _layer_block((D_MODEL, D_FF), l),
            _layer_block((D_MODEL, D_FF), l),
            _layer_block((D_FF, D_MODEL), l),
            _layer_block((1, D_MODEL), l),
            _layer_block((1, D_MODEL), l),
        ],
        out_specs=pl.BlockSpec((TM, D_MODEL), lambda j: (j, 0)),
        out_shape=jax.ShapeDtypeStruct((rows, D_MODEL), F32),
        compiler_params=pltpu.CompilerParams(dimension_semantics=("arbitrary",), vmem_limit_bytes=VMEM_LIMIT),
        name="tail",
    )(*xs, *acts, mod_all, wts["w_out_a"], wts["w_out_b"], wts["ln1_g"], wts["ln1_b"], wts["w1"], wts["w3"],
      wts["w2"], wts["ln2_g"], wts["ln2_b"])


def _rope_tables(L):
    pos = jnp.arange(L)
    row = (pos // GRID_W).astype(F32)
    col = (pos % GRID_W).astype(F32)
    n_freq = QK_ROPE // 4
    inv_freq = ROPE_BASE ** (-jnp.arange(n_freq, dtype=F32) / n_freq)
    ar = row[:, None] * inv_freq
    ac = col[:, None] * inv_freq
    ones = jnp.ones((L, QK_NOPE), F32)
    zeros = jnp.zeros((L, QK_NOPE), F32)
    pad1 = jnp.ones((L, HEAD_SLAB - QK_NOPE - QK_ROPE), F32)
    pad0 = jnp.zeros((L, HEAD_SLAB - QK_NOPE - QK_ROPE), F32)
    cos = jnp.concatenate([ones, jnp.cos(ar), jnp.cos(ar), jnp.cos(ac), jnp.cos(ac), pad1], axis=1)
    sin = jnp.concatenate([zeros, -jnp.sin(ar), jnp.sin(ar), -jnp.sin(ac), jnp.sin(ac), pad0], axis=1)
    cos = jnp.concatenate([cos, jnp.ones((TM, HEAD_SLAB), F32)], axis=0)
    sin = jnp.concatenate([sin, jnp.zeros((TM, HEAD_SLAB), F32)], axis=0)
    return cos, sin


def _dft_tables(seq):
    idx = np.arange(seq, dtype=np.int64)
    ang = 2.0 * np.pi * ((idx[:, None] * idx[None, :]) % seq).astype(np.float64) / seq
    dft = np.concatenate([np.cos(ang), -np.sin(ang)], axis=1)
    ch = FOURIER_DIM // FOURIER_HEADS
    cidx = np.arange(ch, dtype=np.int64)
    cang = 2.0 * np.pi * ((cidx[:, None] * cidx[None, :]) % ch).astype(np.float64) / ch
    norm = 1.0 / np.sqrt(float(seq) * ch)
    eye = np.eye(FOURIER_HEADS)
    cbd = np.kron(eye, np.cos(cang)) * norm
    sbd = np.kron(eye, np.sin(cang)) * norm
    as_bf16 = lambda a: jnp.asarray(a.astype(np.float32)).astype(BF16)
    return as_bf16(dft), as_bf16(cbd), as_bf16(sbd)


def _pool_inv_counts(seq):
    t = np.arange(seq)
    cols = []
    for w in POOL_WINDOWS:
        lo = np.clip(t - w // 2, 0, seq)
        hi = np.clip(t + w // 2, 0, seq)
        cols.append(np.repeat((1.0 / (hi - lo).astype(np.float64))[:, None], POOL_GDIM, axis=1))
    return jnp.asarray(np.concatenate(cols, axis=1), dtype=F32)


def _seq_tables(seq):
    dft, cbd, sbd = _dft_tables(seq)
    return {"dft": dft, "cbd": cbd, "sbd": sbd, "icnt": _pool_inv_counts(seq)}


def _prep_weights(w_in, q_norm, w_uq, kv_norm, w_uk, w_uv, sgu_ln_g, sgu_ln_b, w_spatial, b_spatial, w_pool,
                  pool_scale, w_fourier, w_out, ln1_g, ln1_b, w_ffn1, w_ffn3, w_ffn2, ln2_g, ln2_b):
    rope_dim = QK_NOPE + QK_ROPE
    slab_pad = HEAD_SLAB - rope_dim
    w_in_p = jnp.concatenate(
        [w_in[..., :OFF_SGU], jnp.zeros((DEPTH, D_MODEL, PROJ_A - OFF_SGU), F32), w_in[..., OFF_SGU:]], axis=-1)
    e = QK_ROPE // 4
    swap = np.concatenate([np.arange(e, 2 * e), np.arange(0, e), np.arange(3 * e, 4 * e), np.arange(2 * e, 3 * e)])
    wq4 = w_uq.reshape(DEPTH, Q_LORA, N_HEADS, rope_dim)
    wq = jnp.pad(wq4, ((0, 0), (0, 0), (0, 0), (0, slab_pad)))
    wq_sw = jnp.concatenate([jnp.zeros((DEPTH, Q_LORA, N_HEADS, QK_NOPE), F32), wq4[..., QK_NOPE:][..., swap],
                             jnp.zeros((DEPTH, Q_LORA, N_HEADS, slab_pad), F32)], axis=-1)
    krows = PROJ_A - Q_LORA
    wk_nope = jnp.pad(w_uk.reshape(DEPTH, KV_LORA, N_HEADS, QK_NOPE), ((0, 0), (0, 0), (0, 0), (0, HEAD_SLAB - QK_NOPE)))
    place = np.zeros((QK_ROPE, HEAD_SLAB), np.float32)
    place[np.arange(QK_ROPE), QK_NOPE + np.arange(QK_ROPE)] = 1.0
    place_sw = np.zeros((QK_ROPE, HEAD_SLAB), np.float32)
    place_sw[swap, QK_NOPE + np.arange(QK_ROPE)] = 1.0
    place_all = jnp.broadcast_to(jnp.asarray(np.tile(place, (1, N_HEADS)))[None], (DEPTH, QK_ROPE, N_HEADS * HEAD_SLAB))
    place_sw_all = jnp.broadcast_to(jnp.asarray(np.tile(place_sw, (1, N_HEADS)))[None],
                                    (DEPTH, QK_ROPE, N_HEADS * HEAD_SLAB))
    ztail = jnp.zeros((DEPTH, krows - KV_LORA - QK_ROPE, N_HEADS * HEAD_SLAB), F32)
    wk = jnp.concatenate([wk_nope.reshape(DEPTH, KV_LORA, N_HEADS * HEAD_SLAB), place_all, ztail], axis=1)
    wk_sw = jnp.concatenate([jnp.zeros((DEPTH, KV_LORA, N_HEADS * HEAD_SLAB), F32), place_sw_all, ztail], axis=1)
    w_uvt = jnp.pad(jnp.transpose(w_uv, (0, 2, 1)).reshape(DEPTH, N_HEADS, V_DIM, KV_LORA),
                    ((0, 0), (0, 0), (0, V_SLAB - V_DIM), (0, 0))).reshape(DEPTH, VT_ROWS, KV_LORA)
    v_ones = np.zeros((N_HEADS, V_SLAB, 1), np.float32)
    v_ones[:, V_DIM:] = 1.0
    ws_cat = jnp.transpose(w_spatial, (0, 2, 1, 3)).reshape(DEPTH, CHUNK, SGU_HEADS * CHUNK)
    sgu_bias = jnp.repeat(jnp.transpose(b_spatial, (0, 2, 1)), SGU_DIM // SGU_HEADS, axis=2)
    wp_bd = jnp.zeros((DEPTH, POOL_DIM, POOL_DIM), F32)
    for gi in range(len(POOL_WINDOWS)):
        sl = slice(gi * POOL_GDIM, (gi + 1) * POOL_GDIM)
        wp_bd = wp_bd.at[:, sl, sl].set(w_pool[:, gi])
    bf = lambda a: a.astype(BF16)
    row = lambda a: a.reshape(DEPTH, 1, a.shape[-1])
    return {
        "w_in": bf(w_in_p), "q_norm": row(q_norm), "kv_norm": row(kv_norm),
        "wq": bf(wq.reshape(DEPTH, Q_LORA, N_HEADS * HEAD_SLAB)),
        "wq_sw": bf(wq_sw.reshape(DEPTH, Q_LORA, N_HEADS * HEAD_SLAB)),
        "wk": bf(wk), "wk_sw": bf(wk_sw), "w_uvt": bf(w_uvt), "v_ones": jnp.asarray(v_ones.reshape(VT_ROWS, 1)),
        "sgu_g": row(sgu_ln_g), "sgu_b": row(sgu_ln_b), "ws_cat": bf(ws_cat), "sgu_bias": sgu_bias,
        "wp_bd": bf(wp_bd), "pool_scale": row(pool_scale), "w_f": bf(w_fourier),
        "w_out_a": bf(w_out[:, :ATTN_DIM]), "w_out_b": bf(w_out[:, ATTN_DIM:]),
        "ln1_g": row(ln1_g), "ln1_b": row(ln1_b), "w1": bf(w_ffn1), "w3": bf(w_ffn3), "w2": bf(w_ffn2),
        "ln2_g": row(ln2_g), "ln2_b": row(ln2_b),
    }


def kernel(x, c, ctx, c_ctx, w_mod, b_mod, w_in, q_norm, w_uq, kv_norm, w_uk, w_uv, sgu_ln_g, sgu_ln_b, w_spatial,
           b_spatial, w_pool, pool_scale, w_fourier, w_out, ln1_g, ln1_b, w_ffn1, w_ffn3, w_ffn2, ln2_g, ln2_b):
    B, L, D = x.shape
    Lc = ctx.shape[1]
    assert D == D_MODEL and L % TM == 0 and L % TQ == 0 and L % GRID_W == 0 and L % Lc == 0
    assert TM % Lc == 0 and (B * Lc) % TM == 0 and Lc % CHUNK == 0 and B < MOD_ROWS
    dims = (B, L, Lc)
    r_lat = B * L
    xs = (x.reshape(r_lat, D), ctx.reshape(B * Lc, D))

    cc = jnp.concatenate([c, c_ctx[None, :], jnp.zeros((MOD_ROWS - B - 1, D), F32)], axis=0)
    mod_all = _modulation(cc, w_mod, b_mod).reshape(DEPTH, MOD_ROWS, 1, 6 * D)

    wts = _prep_weights(w_in, q_norm, w_uq, kv_norm, w_uk, w_uv, sgu_ln_g, sgu_ln_b, w_spatial, b_spatial, w_pool,
                        pool_scale, w_fourier, w_out, ln1_g, ln1_b, w_ffn1, w_ffn3, w_ffn2, ln2_g, ln2_b)
    cos_t, sin_t = _rope_tables(L)
    tabs_lat = _seq_tables(L)
    tabs_ctx = _seq_tables(Lc)

    for l in range(DEPTH):
        last = l == DEPTH - 1
        q, k, vt, loc_in, pool_in = _inproj(l, xs, mod_all, cos_t, sin_t, wts, dims)
        attn = _attention_latent(q, k, vt, dims)
        loc = _local_mixers(l, loc_in, pool_in, tabs_lat, wts, L, 0, B)
        ctx_parts = None
        if not last:
            ctx_parts = (_attention_context(q, k, vt, dims),
                         _local_mixers(l, loc_in, pool_in, tabs_ctx, wts, Lc, r_lat // Lc, B))
        xs = (_tail(l, xs, attn, loc, ctx_parts, mod_all, wts, dims),)
    return xs[0].reshape(B, L, D)
```

```python
import functools
import math

import numpy as np
import jax
import jax.numpy as jnp
from jax import lax
from jax.experimental import pallas as pl
from jax.experimental.pallas import tpu as pltpu

F32 = jnp.float32
BF16 = jnp.bfloat16

D_MODEL = 1024
DEPTH = 4
GRID_W = 64
N_HEADS = 8
QK_NOPE = 64
QK_ROPE = 32
V_DIM = 64
Q_LORA = 256
KV_LORA = 128
ROPE_BASE = 10000.0
SGU_DIM = 256
SGU_HEADS = 4
CHUNK = 128
POOL_DIM = 256
POOL_WINDOWS = (2, 4, 8, 16)
POOL_GDIM = POOL_DIM // len(POOL_WINDOWS)
FOURIER_DIM = 256
FOURIER_HEADS = 4
ATTN_DIM = N_HEADS * V_DIM
LOC_DIM = SGU_DIM + POOL_DIM + FOURIER_DIM
LOC_IN = 2 * SGU_DIM + FOURIER_DIM
OFF_SGU = Q_LORA + KV_LORA + QK_ROPE
IN_DIM = OFF_SGU + 2 * SGU_DIM + POOL_DIM + FOURIER_DIM
D_FF = ((8 * D_MODEL + 3 * 256 - 1) // (3 * 256)) * 256
LN_EPS = 1e-6
DEEPNORM_ALPHA = (2.0 * DEPTH) ** 0.25
QK_SCALE_LOG2 = (QK_NOPE + QK_ROPE) ** -0.5 * math.log2(math.e)
SGU_GRP_SHIFT = (SGU_DIM // SGU_HEADS).bit_length() - 1
POOL_GRP_SHIFT = POOL_GDIM.bit_length() - 1

LANE = 128
BF16_SUBLANES = 16
HEAD_SLAB = LANE
V_SLAB = V_DIM + BF16_SUBLANES
VT_ROWS = N_HEADS * V_SLAB
PROJ_A = 512
PROJ_W = PROJ_A + 2 * SGU_DIM + POOL_DIM + FOURIER_DIM
POOL_PAD = 16
VMEM_LIMIT = 56 * 1024 * 1024

TM = 512
TQ = 512
KEY_CHUNK = 512
TAIL_SPLIT = 2
SGU_UNROLL = 4
MOD_NB = 1536
MOD_ROWS = 16


def _layer_block(shape, l):
    n = len(shape)
    return pl.BlockSpec((None,) + tuple(shape), lambda *_: (l,) + (0,) * n, pipeline_mode=pl.Buffered(1))


def _resident(shape):
    n = len(shape)
    return pl.BlockSpec(shape, lambda *_: (0,) * n, pipeline_mode=pl.Buffered(1))


def _dot(a, b):
    return jnp.dot(a, b, preferred_element_type=F32)


def _dot_nt(a, b):
    return lax.dot_general(a, b, (((1,), (1,)), ((), ())), preferred_element_type=F32)


def _layer_norm(x, g, b):
    mu = jnp.mean(x, axis=-1, keepdims=True)
    xc = x - mu
    var = jnp.mean(xc * xc, axis=-1, keepdims=True)
    return xc * lax.rsqrt(var + LN_EPS) * g + b


def _rms_norm(x, g):
    return x * lax.rsqrt(jnp.mean(x * x, axis=-1, keepdims=True) + LN_EPS) * g


def _mod_kernel(c_ref, w_ref, b_ref, o_ref):
    c = c_ref[...]
    s = c * (1.0 / (1.0 + jnp.exp(-c)))
    o_ref[0] = _dot(s.astype(BF16), w_ref[0].astype(BF16)) + b_ref[0]


def _modulation(cc, w_mod, b_mod):
    d6 = w_mod.shape[-1]
    return pl.pallas_call(
        _mod_kernel,
        grid=(DEPTH, d6 // MOD_NB),
        in_specs=[
            pl.BlockSpec((MOD_ROWS, D_MODEL), lambda l, n: (0, 0)),
            pl.BlockSpec((1, D_MODEL, MOD_NB), lambda l, n: (l, 0, n)),
            pl.BlockSpec((1, 1, MOD_NB), lambda l, n: (l, 0, n)),
        ],
        out_specs=pl.BlockSpec((1, MOD_ROWS, MOD_NB), lambda l, n: (l, 0, n)),
        out_shape=jax.ShapeDtypeStruct((DEPTH, MOD_ROWS, d6), F32),
        compiler_params=pltpu.CompilerParams(dimension_semantics=("arbitrary", "arbitrary")),
        name="modulation",
    )(cc, w_mod, b_mod.reshape(DEPTH, 1, d6))


def _mod_spec(l, npb, B):
    return pl.BlockSpec((None, 1, 1, 6 * D_MODEL), lambda j: (l, jnp.minimum(j // npb, B), 0, 0))


def _token_rows(x_refs, nlat):
    if len(x_refs) == 1:
        return x_refs[0][...]
    return jnp.where(pl.program_id(0) < nlat, x_refs[0][...], x_refs[1][...])


def _token_specs(xs, nlat):
    if len(xs) == 1:
        return [pl.BlockSpec((TM, D_MODEL), lambda j: (j, 0))]
    return [pl.BlockSpec((TM, D_MODEL), lambda j: (jnp.minimum(j, nlat - 1), 0)),
            pl.BlockSpec((TM, D_MODEL), lambda j: (jnp.maximum(j - nlat, 0), 0))]


def _inproj_kernel(*refs, nx, nlat):
    (mod_ref, cos_ref, sin_ref, win_ref, qn_ref, kvn_ref, wq_ref, wqs_ref,
     wk_ref, wks_ref, wuvt_ref, vones_ref, q_ref, k_ref, vt_ref, loc_ref, pool_ref) = refs[nx:]
    x = _token_rows(refs[:nx], nlat)
    sh1 = mod_ref[0, :, 0:D_MODEL]
    sc1 = mod_ref[0, :, D_MODEL:2 * D_MODEL]
    h = (x * (1.0 + sc1) + sh1).astype(BF16)
    proj = _dot(h, win_ref[...])
    cqn = _rms_norm(proj[:, 0:Q_LORA], qn_ref[...]).astype(BF16)
    ckvn = _rms_norm(proj[:, Q_LORA:Q_LORA + KV_LORA], kvn_ref[...]).astype(BF16)
    xk = jnp.concatenate([ckvn, proj[:, Q_LORA + KV_LORA:PROJ_A].astype(BF16)], axis=1)
    q = _dot(cqn, wq_ref[...])
    qs = _dot(cqn, wqs_ref[...])
    k = _dot(xk, wk_ref[...])
    ks = _dot(xk, wks_ref[...])
    cos = cos_ref[...]
    sin = sin_ref[...]
    cos_q = cos * QK_SCALE_LOG2
    sin_q = sin * QK_SCALE_LOG2
    for hd in range(N_HEADS):
        sl = slice(hd * HEAD_SLAB, (hd + 1) * HEAD_SLAB)
        q_ref[hd] = (q[:, sl] * cos_q + qs[:, sl] * sin_q).astype(BF16)
        k_ref[hd] = (k[:, sl] * cos + ks[:, sl] * sin).astype(BF16)
    vt_ref[...] = (_dot_nt(wuvt_ref[...], ckvn) + vones_ref[...]).astype(BF16)
    loc_ref[:, 0:2 * SGU_DIM] = proj[:, PROJ_A:PROJ_A + 2 * SGU_DIM].astype(BF16)
    loc_ref[:, 2 * SGU_DIM:] = proj[:, PROJ_W - FOURIER_DIM:].astype(BF16)
    pool_ref[...] = proj[:, PROJ_A + 2 * SGU_DIM:PROJ_A + 2 * SGU_DIM + POOL_DIM]


def _inproj(l, xs, mod_all, cos_t, sin_t, wts, dims):
    B, L, Lc = dims
    R = B * (L + Lc)
    nlat = B * L // TM
    npb = L // TM

    def tab_idx(j):
        return (jnp.where(j < nlat, j % npb, npb), 0)

    return pl.pallas_call(
        functools.partial(_inproj_kernel, nx=len(xs), nlat=nlat),
        grid=(R // TM,),
        in_specs=_token_specs(xs, nlat) + [
            _mod_spec(l, npb, B),
            pl.BlockSpec((TM, HEAD_SLAB), tab_idx),
            pl.BlockSpec((TM, HEAD_SLAB), tab_idx),
            _layer_block((D_MODEL, PROJ_W), l),
            _layer_block((1, Q_LORA), l),
            _layer_block((1, KV_LORA), l),
            _layer_block((Q_LORA, N_HEADS * HEAD_SLAB), l),
            _layer_block((Q_LORA, N_HEADS * HEAD_SLAB), l),
            _layer_block((PROJ_A - Q_LORA, N_HEADS * HEAD_SLAB), l),
            _layer_block((PROJ_A - Q_LORA, N_HEADS * HEAD_SLAB), l),
            _layer_block((VT_ROWS, KV_LORA), l),
            _resident((VT_ROWS, 1)),
        ],
        out_specs=[
            pl.BlockSpec((N_HEADS, TM, HEAD_SLAB), lambda j: (0, j, 0)),
            pl.BlockSpec((N_HEADS, TM, HEAD_SLAB), lambda j: (0, j, 0)),
            pl.BlockSpec((VT_ROWS, TM), lambda j: (0, j)),
            pl.BlockSpec((TM, LOC_IN), lambda j: (j, 0)),
            pl.BlockSpec((TM, POOL_DIM), lambda j: (j, 0)),
        ],
        out_shape=[
            jax.ShapeDtypeStruct((N_HEADS, R, HEAD_SLAB), BF16),
            jax.ShapeDtypeStruct((N_HEADS, R, HEAD_SLAB), BF16),
            jax.ShapeDtypeStruct((VT_ROWS, R), BF16),
            jax.ShapeDtypeStruct((R, LOC_IN), BF16),
            jax.ShapeDtypeStruct((R, POOL_DIM), F32),
        ],
        compiler_params=pltpu.CompilerParams(dimension_semantics=("arbitrary",), vmem_limit_bytes=VMEM_LIMIT),
        name="inproj",
    )(*xs, mod_all, cos_t, sin_t, wts["w_in"], wts["q_norm"], wts["kv_norm"], wts["wq"], wts["wq_sw"],
      wts["wk"], wts["wk_sw"], wts["w_uvt"], wts["v_ones"])


def _attn_kernel(*refs, with_latent, tq, nq, kc):
    if with_latent:
        (q_ref, kl_ref, kc_ref, vl_ref, vc_ref, o_ref, acc_ref, sla_ref, sca_ref, ma_ref, slb_ref, scb_ref,
         mb_ref) = refs
        bufs = ((sla_ref, sca_ref, ma_ref), (slb_ref, scb_ref, mb_ref))
        n_lat = kl_ref.shape[1] // kc
    else:
        q_ref, kc_ref, vc_ref, o_ref, acc_ref, sca_ref, ma_ref, scb_ref, mb_ref = refs
        bufs = ((None, sca_ref, ma_ref), (None, scb_ref, mb_ref))
        n_lat = 0
    head_bits = N_HEADS.bit_length() - 1
    assert N_HEADS == 1 << head_bits and N_HEADS % 2 == 0
    n_stage = nq * N_HEADS

    def stage(st_soft, buf_soft, st_score, buf_score):
        if st_score is not None:
            hd2 = st_score & (N_HEADS - 1)
            q0 = pl.multiple_of((st_score >> head_bits) * tq, tq)
            qh = q_ref[hd2, pl.ds(q0, tq), :]
            m_new = None
        if st_soft is not None:
            hd = st_soft & (N_HEADS - 1)
            r0 = pl.multiple_of(hd * V_SLAB, BF16_SUBLANES)
            m = buf_soft[2][...]
            o = None
        for c in range(n_lat + 1):
            if st_score is not None:
                if c == 0:
                    s_c = _dot_nt(kc_ref[hd2], qh)
                    buf_score[1][...] = s_c
                else:
                    s_c = _dot_nt(kl_ref[hd2, pl.ds((c - 1) * kc, kc), :], qh)
                    buf_score[0][pl.ds((c - 1) * kc, kc), :] = s_c
                m_c = jnp.max(s_c, axis=0, keepdims=True)
                m_new = m_c if m_new is None else jnp.maximum(m_new, m_c)
            if st_soft is not None:
                if c == 0:
                    p_c = jnp.exp2(buf_soft[1][...] - m).astype(BF16)
                    o_c = _dot(vc_ref[pl.ds(r0, V_SLAB), :], p_c)
                else:
                    p_c = jnp.exp2(buf_soft[0][pl.ds((c - 1) * kc, kc), :] - m).astype(BF16)
                    o_c = _dot(vl_ref[pl.ds(r0, V_SLAB), pl.ds((c - 1) * kc, kc)], p_c)
                o = o_c if o is None else o + o_c
        if st_score is not None:
            buf_score[2][...] = m_new
        if st_soft is not None:
            a0 = pl.multiple_of(hd * V_DIM, V_DIM)
            acc_ref[pl.ds(a0, V_DIM), :] = o[0:V_DIM] / o[V_DIM:V_DIM + 1]

    def flush(qb):
        o0 = pl.multiple_of(qb * tq, tq)
        o_ref[pl.ds(o0, tq), :] = acc_ref[...].T.astype(BF16)

    stage(None, None, 0, bufs[0])

    def pair(t, carry):
        s0 = 2 * t
        stage(s0, bufs[0], s0 + 1, bufs[1])
        stage(s0 + 1, bufs[1], s0 + 2, bufs[0])

        @pl.when((s0 + 2) & (N_HEADS - 1) == 0)
        def _():
            flush(s0 >> head_bits)

        return carry

    lax.fori_loop(0, n_stage // 2 - 1, pair, 0)
    stage(n_stage - 2, bufs[0], n_stage - 1, bufs[1])
    stage(n_stage - 1, bufs[1], None, None)
    flush(nq - 1)


def _attention_latent(q, k, vt, dims):
    B, L, Lc = dims
    ctx_blk0 = B * L // Lc
    return pl.pallas_call(
        functools.partial(_attn_kernel, with_latent=True, tq=TQ, nq=L // TQ, kc=KEY_CHUNK),
        grid=(B,),
        in_specs=[
            pl.BlockSpec((N_HEADS, L, HEAD_SLAB), lambda b: (0, b, 0)),
            pl.BlockSpec((N_HEADS, L, HEAD_SLAB), lambda b: (0, b, 0)),
            pl.BlockSpec((N_HEADS, Lc, HEAD_SLAB), lambda b: (0, ctx_blk0 + b, 0)),
            pl.BlockSpec((VT_ROWS, L), lambda b: (0, b)),
            pl.BlockSpec((VT_ROWS, Lc), lambda b: (0, ctx_blk0 + b)),
        ],
        out_specs=pl.BlockSpec((L, ATTN_DIM), lambda b: (b, 0)),
        out_shape=jax.ShapeDtypeStruct((B * L, ATTN_DIM), BF16),
        scratch_shapes=[pltpu.VMEM((ATTN_DIM, TQ), F32)]
        + [pltpu.VMEM((L, TQ), F32), pltpu.VMEM((Lc, TQ), F32), pltpu.VMEM((1, TQ), F32)] * 2,
        compiler_params=pltpu.CompilerParams(dimension_semantics=("arbitrary",), vmem_limit_bytes=VMEM_LIMIT),
        name="attn_latent",
    )(q, k, k, vt, vt)


def _attention_context(q, k, vt, dims):
    B, L, Lc = dims
    ctx_blk0 = B * L // Lc
    return pl.pallas_call(
        functools.partial(_attn_kernel, with_latent=False, tq=Lc, nq=1, kc=KEY_CHUNK),
        grid=(B,),
        in_specs=[
            pl.BlockSpec((N_HEADS, Lc, HEAD_SLAB), lambda b: (0, ctx_blk0 + b, 0)),
            pl.BlockSpec((N_HEADS, Lc, HEAD_SLAB), lambda b: (0, ctx_blk0 + b, 0)),
            pl.BlockSpec((VT_ROWS, Lc), lambda b: (0, ctx_blk0 + b)),
        ],
        out_specs=pl.BlockSpec((Lc, ATTN_DIM), lambda b: (b, 0)),
        out_shape=jax.ShapeDtypeStruct((B * Lc, ATTN_DIM), BF16),
        scratch_shapes=[pltpu.VMEM((ATTN_DIM, Lc), F32)] + [pltpu.VMEM((Lc, Lc), F32), pltpu.VMEM((1, Lc), F32)] * 2,
        compiler_params=pltpu.CompilerParams(dimension_semantics=("arbitrary",)),
        name="attn_context",
    )(q, k, vt)


def _local_kernel(loc_ref, pool_ref, dft_ref, cbd_ref, sbd_ref, wf_ref, sg_ref, sb_ref, ws_ref, bias_ref,
                  wp_ref, ps_ref, icnt_ref, o_ref, bufa, bufb, *, seq):
    lane_grp = lax.shift_right_logical(lax.broadcasted_iota(jnp.int32, (CHUNK, SGU_DIM), 1), SGU_GRP_SHIFT)
    sg = sg_ref[...]
    sb = sb_ref[...]

    def chunk(ci, carry):
        r0 = pl.multiple_of(ci * CHUNK, CHUNK)
        u = loc_ref[pl.ds(r0, CHUNK), 0:SGU_DIM].astype(F32)
        v = loc_ref[pl.ds(r0, CHUNK), SGU_DIM:2 * SGU_DIM].astype(F32)
        vn = _layer_norm(v, sg, sb)
        stacked = jnp.concatenate(
            [jnp.where(lane_grp == g, vn, 0.0).astype(BF16) for g in range(SGU_HEADS)], axis=0)
        mixed = _dot(ws_ref[...], stacked) + bias_ref[...]
        o_ref[pl.ds(r0, CHUNK), 0:SGU_DIM] = (u * mixed).astype(BF16)
        return carry

    lax.fori_loop(0, seq // CHUNK, chunk, 0, unroll=min(SGU_UNROLL, seq // CHUNK))

    x = pool_ref[...]
    n = seq + POOL_PAD
    half = POOL_PAD // 2
    zpad = jnp.zeros((POOL_PAD, POOL_DIM), F32)
    bufa[0:POOL_PAD, :] = zpad
    bufa[seq + POOL_PAD:seq + 2 * POOL_PAD, :] = zpad
    bufa[POOL_PAD:seq + POOL_PAD, :] = x
    bufb[0:half, :] = zpad[0:half]
    bufb[n + half:n + POOL_PAD, :] = zpad[0:half]
    c2 = bufa[pl.ds(half - 1, n), :] + bufa[pl.ds(half, n), :]
    bufb[pl.ds(half, n), :] = c2
    c4 = bufb[pl.ds(half - 1, n), :] + bufb[pl.ds(half + 1, n), :]
    bufa[pl.ds(half, n), :] = c4
    c8 = bufa[pl.ds(half - 2, n), :] + bufa[pl.ds(half + 2, n), :]
    bufb[pl.ds(half, n), :] = c8
    c16 = bufb[pl.ds(half - 4, n), :] + bufb[pl.ds(half + 4, n), :]
    grp = lax.shift_right_logical(lax.broadcasted_iota(jnp.int32, (seq, POOL_DIM), 1), POOL_GRP_SHIFT)
    inner = slice(half, half + seq)
    wsum = jnp.where(grp == 0, c2[inner], jnp.where(grp == 1, c4[inner], jnp.where(grp == 2, c8[inner], c16[inner])))
    centred = (wsum * icnt_ref[...] - x).astype(BF16)
    o_ref[:, SGU_DIM:SGU_DIM + POOL_DIM] = (_dot(centred, wp_ref[...]) * ps_ref[...]).astype(BF16)

    f = loc_ref[:, 2 * SGU_DIM:]
    g = jnp.concatenate([_dot(f, cbd_ref[...]).astype(BF16), _dot(f, sbd_ref[...]).astype(BF16)], axis=0)
    spec = _dot(dft_ref[...], g)
    o_ref[:, SGU_DIM + POOL_DIM:] = _dot(spec.astype(BF16), wf_ref[...]).astype(BF16)


def _local_mixers(l, loc_in, pool_in, tabs, wts, seq, blk0, nseq):
    return pl.pallas_call(
        functools.partial(_local_kernel, seq=seq),
        grid=(nseq,),
        in_specs=[
            pl.BlockSpec((seq, LOC_IN), lambda b: (blk0 + b, 0)),
            pl.BlockSpec((seq, POOL_DIM), lambda b: (blk0 + b, 0)),
            _resident((seq, 2 * seq)),
            _resident((FOURIER_DIM, FOURIER_DIM)),
            _resident((FOURIER_DIM, FOURIER_DIM)),
            _layer_block((FOURIER_DIM, FOURIER_DIM), l),
            _layer_block((1, SGU_DIM), l),
            _layer_block((1, SGU_DIM), l),
            _layer_block((CHUNK, SGU_HEADS * CHUNK), l),
            _layer_block((CHUNK, SGU_DIM), l),
            _layer_block((POOL_DIM, POOL_DIM), l),
            _layer_block((1, POOL_DIM), l),
            _resident((seq, POOL_DIM)),
        ],
        out_specs=pl.BlockSpec((seq, LOC_DIM), lambda b: (b, 0)),
        out_shape=jax.ShapeDtypeStruct((nseq * seq, LOC_DIM), BF16),
        scratch_shapes=[pltpu.VMEM((seq + 2 * POOL_PAD, POOL_DIM), F32),
                        pltpu.VMEM((seq + 2 * POOL_PAD, POOL_DIM), F32)],
        compiler_params=pltpu.CompilerParams(dimension_semantics=("arbitrary",), vmem_limit_bytes=VMEM_LIMIT),
        name="local_mixers_%d" % seq,
    )(loc_in, pool_in, tabs["dft"], tabs["cbd"], tabs["sbd"], wts["w_f"], wts["sgu_g"], wts["sgu_b"],
      wts["ws_cat"], wts["sgu_bias"], wts["wp_bd"], wts["pool_scale"], tabs["icnt"])


def _tail_kernel(*refs, nx, nlat, with_ctx):
    x_refs = refs[:nx]
    refs = refs[nx:]
    if with_ctx:
        (attn_ref, attnc_ref, loc_ref, locc_ref, mod_ref, woa_ref, wob_ref, l1g_ref, l1b_ref, w1_ref, w3_ref,
         w2_ref, l2g_ref, l2b_ref, o_ref) = refs
    else:
        (attn_ref, loc_ref, mod_ref, woa_ref, wob_ref, l1g_ref, l1b_ref, w1_ref, w3_ref, w2_ref,
         l2g_ref, l2b_ref, o_ref) = refs
        attnc_ref = locc_ref = None
    is_lat = pl.program_id(0) < nlat

    def rows(lat_ref, ctx_ref, rs):
        if ctx_ref is None:
            return lat_ref[rs, :]
        return jnp.where(is_lat, lat_ref[rs, :], ctx_ref[rs, :])

    g1 = mod_ref[0, :, 2 * D_MODEL:3 * D_MODEL]
    sh2 = mod_ref[0, :, 3 * D_MODEL:4 * D_MODEL]
    sc2 = mod_ref[0, :, 4 * D_MODEL:5 * D_MODEL]
    g2 = mod_ref[0, :, 5 * D_MODEL:6 * D_MODEL]
    sub = TM // TAIL_SPLIT
    parts = [pl.ds(p * sub, sub) for p in range(TAIL_SPLIT)]
    ys = []
    for rs in parts:
        attn = rows(attn_ref, attnc_ref, rs)
        loc = rows(loc_ref, locc_ref, rs)
        ys.append(_dot(attn, woa_ref[...]) + _dot(loc, wob_ref[...]))
    x1s, ups = [], []
    for rs, y in zip(parts, ys):
        x = rows(x_refs[0], x_refs[1] if nx == 2 else None, rs)
        x1 = _layer_norm(DEEPNORM_ALPHA * x + g1 * y, l1g_ref[...], l1b_ref[...])
        h = (x1 * (1.0 + sc2) + sh2).astype(BF16)
        x1s.append(x1)
        ups.append((_dot(h, w1_ref[...]), _dot(h, w3_ref[...])))
    ffns = []
    for a, b in ups:
        hid = (a * (1.0 / (1.0 + jnp.exp(-a))) * b).astype(BF16)
        ffns.append(_dot(hid, w2_ref[...]))
    for rs, x1, ffn in zip(parts, x1s, ffns):
        o_ref[rs, :] = _layer_norm(DEEPNORM_ALPHA * x1 + g2 * ffn, l2g_ref[...], l2b_ref[...])


def _tail(l, xs, attn, loc, ctx_parts, mod_all, wts, dims):
    B, L, Lc = dims
    npb = L // TM
    nlat = B * L // TM
    rows = B * L if ctx_parts is None else B * (L + Lc)
    lat_idx = lambda j: (jnp.minimum(j, nlat - 1), 0)
    ctx_idx = lambda j: (jnp.maximum(j - nlat, 0), 0)
    act_specs = [pl.BlockSpec((TM, ATTN_DIM), lat_idx), pl.BlockSpec((TM, LOC_DIM), lat_idx)]
    acts = [attn, loc]
    if ctx_parts is not None:
        act_specs = [act_specs[0], pl.BlockSpec((TM, ATTN_DIM), ctx_idx), act_specs[1],
                     pl.BlockSpec((TM, LOC_DIM), ctx_idx)]
        acts = [attn, ctx_parts[0], loc, ctx_parts[1]]
    return pl.pallas_call(
        functools.partial(_tail_kernel, nx=len(xs), nlat=nlat, with_ctx=ctx_parts is not None),
        grid=(rows // TM,),
        in_specs=_token_specs(xs, nlat) + act_specs + [
            _mod_spec(l, npb, B),
            _layer_block((ATTN_DIM, D_MODEL), l),
            _layer_block((LOC_DIM, D_MODEL), l),
            _layer_block((1, D_MODEL), l),
            _layer_block((1, D_MODEL), l),
            _layer_block((D_MODEL, D_FF), l),
            _layer_block((D_MODEL, D_FF), l),
            _layer_block((D_FF, D_MODEL), l),
            _layer_block((1, D_MODEL), l),
            _layer_block((1, D_MODEL), l),
        ],
        out_specs=pl.BlockSpec((TM, D_MODEL), lambda j: (j, 0)),
        out_shape=jax.ShapeDtypeStruct((rows, D_MODEL), F32),
        compiler_params=pltpu.CompilerParams(dimension_semantics=("arbitrary",), vmem_limit_bytes=VMEM_LIMIT),
        name="tail",
    )(*xs, *acts, mod_all, wts["w_out_a"], wts["w_out_b"], wts["ln1_g"], wts["ln1_b"], wts["w1"], wts["w3"],
      wts["w2"], wts["ln2_g"], wts["ln2_b"])


def _rope_tables(L):
    pos = jnp.arange(L)
    row = (pos // GRID_W).astype(F32)
    col = (pos % GRID_W).astype(F32)
    n_freq = QK_ROPE // 4
    inv_freq = ROPE_BASE ** (-jnp.arange(n_freq, dtype=F32) / n_freq)
    ar = row[:, None] * inv_freq
    ac = col[:, None] * inv_freq
    ones = jnp.ones((L, QK_NOPE), F32)
    zeros = jnp.zeros((L, QK_NOPE), F32)
    pad1 = jnp.ones((L, HEAD_SLAB - QK_NOPE - QK_ROPE), F32)
    pad0 = jnp.zeros((L, HEAD_SLAB - QK_NOPE - QK_ROPE), F32)
    cos = jnp.concatenate([ones, jnp.cos(ar), jnp.cos(ar), jnp.cos(ac), jnp.cos(ac), pad1], axis=1)
    sin = jnp.concatenate([zeros, -jnp.sin(ar), jnp.sin(ar), -jnp.sin(ac), jnp.sin(ac), pad0], axis=1)
    cos = jnp.concatenate([cos, jnp.ones((TM, HEAD_SLAB), F32)], axis=0)
    sin = jnp.concatenate([sin, jnp.zeros((TM, HEAD_SLAB), F32)], axis=0)
    return cos, sin


def _dft_tables(seq):
    idx = np.arange(seq, dtype=np.int64)
    ang = 2.0 * np.pi * ((idx[:, None] * idx[None, :]) % seq).astype(np.float64) / seq
    dft = np.concatenate([np.cos(ang), -np.sin(ang)], axis=1)
    ch = FOURIER_DIM // FOURIER_HEADS
    cidx = np.arange(ch, dtype=np.int64)
    cang = 2.0 * np.pi * ((cidx[:, None] * cidx[None, :]) % ch).astype(np.float64) / ch
    norm = 1.0 / np.sqrt(float(seq) * ch)
    eye = np.eye(FOURIER_HEADS)
    cbd = np.kron(eye, np.cos(cang)) * norm
    sbd = np.kron(eye, np.sin(cang)) * norm
    as_bf16 = lambda a: jnp.asarray(a.astype(np.float32)).astype(BF16)
    return as_bf16(dft), as_bf16(cbd), as_bf16(sbd)


def _pool_inv_counts(seq):
    t = np.arange(seq)
    cols = []
    for w in POOL_WINDOWS:
        lo = np.clip(t - w // 2, 0, seq)
        hi = np.clip(t + w // 2, 0, seq)
        cols.append(np.repeat((1.0 / (hi - lo).astype(np.float64))[:, None], POOL_GDIM, axis=1))
    return jnp.asarray(np.concatenate(cols, axis=1), dtype=F32)


def _seq_tables(seq):
    dft, cbd, sbd = _dft_tables(seq)
    return {"dft": dft, "cbd": cbd, "sbd": sbd, "icnt": _pool_inv_counts(seq)}


def _prep_weights(w_in, q_norm, w_uq, kv_norm, w_uk, w_uv, sgu_ln_g, sgu_ln_b, w_spatial, b_spatial, w_pool,
                  pool_scale, w_fourier, w_out, ln1_g, ln1_b, w_ffn1, w_ffn3, w_ffn2, ln2_g, ln2_b):
    rope_dim = QK_NOPE + QK_ROPE
    slab_pad = HEAD_SLAB - rope_dim
    w_in_p = jnp.concatenate(
        [w_in[..., :OFF_SGU], jnp.zeros((DEPTH, D_MODEL, PROJ_A - OFF_SGU), F32), w_in[..., OFF_SGU:]], axis=-1)
    e = QK_ROPE // 4
    swap = np.concatenate([np.arange(e, 2 * e), np.arange(0, e), np.arange(3 * e, 4 * e), np.arange(2 * e, 3 * e)])
    wq4 = w_uq.reshape(DEPTH, Q_LORA, N_HEADS, rope_dim)
    wq = jnp.pad(wq4, ((0, 0), (0, 0), (0, 0), (0, slab_pad)))
    wq_sw = jnp.concatenate([jnp.zeros((DEPTH, Q_LORA, N_HEADS, QK_NOPE), F32), wq4[..., QK_NOPE:][..., swap],
                             jnp.zeros((DEPTH, Q_LORA, N_HEADS, slab_pad), F32)], axis=-1)
    krows = PROJ_A - Q_LORA
    wk_nope = jnp.pad(w_uk.reshape(DEPTH, KV_LORA, N_HEADS, QK_NOPE), ((0, 0), (0, 0), (0, 0), (0, HEAD_SLAB - QK_NOPE)))
    place = np.zeros((QK_ROPE, HEAD_SLAB), np.float32)
    place[np.arange(QK_ROPE), QK_NOPE + np.arange(QK_ROPE)] = 1.0
    place_sw = np.zeros((QK_ROPE, HEAD_SLAB), np.float32)
    place_sw[swap, QK_NOPE + np.arange(QK_ROPE)] = 1.0
    place_all = jnp.broadcast_to(jnp.asarray(np.tile(place, (1, N_HEADS)))[None], (DEPTH, QK_ROPE, N_HEADS * HEAD_SLAB))
    place_sw_all = jnp.broadcast_to(jnp.asarray(np.tile(place_sw, (1, N_HEADS)))[None],
                                    (DEPTH, QK_ROPE, N_HEADS * HEAD_SLAB))
    ztail = jnp.zeros((DEPTH, krows - KV_LORA - QK_ROPE, N_HEADS * HEAD_SLAB), F32)
    wk = jnp.concatenate([wk_nope.reshape(DEPTH, KV_LORA, N_HEADS * HEAD_SLAB), place_all, ztail], axis=1)
    wk_sw = jnp.concatenate([jnp.zeros((DEPTH, KV_LORA, N_HEADS * HEAD_SLAB), F32), place_sw_all, ztail], axis=1)
    w_uvt = jnp.pad(jnp.transpose(w_uv, (0, 2, 1)).reshape(DEPTH, N_HEADS, V_DIM, KV_LORA),
                    ((0, 0), (0, 0), (0, V_SLAB - V_DIM), (0, 0))).reshape(DEPTH, VT_ROWS, KV_LORA)
    v_ones = np.zeros((N_HEADS, V_SLAB, 1), np.float32)
    v_ones[:, V_DIM:] = 1.0
    ws_cat = jnp.transpose(w_spatial, (0, 2, 1, 3)).reshape(DEPTH, CHUNK, SGU_HEADS * CHUNK)
    sgu_bias = jnp.repeat(jnp.transpose(b_spatial, (0, 2, 1)), SGU_DIM // SGU_HEADS, axis=2)
    wp_bd = jnp.zeros((DEPTH, POOL_DIM, POOL_DIM), F32)
    for gi in range(len(POOL_WINDOWS)):
        sl = slice(gi * POOL_GDIM, (gi + 1) * POOL_GDIM)
        wp_bd = wp_bd.at[:, sl, sl].set(w_pool[:, gi])
    bf = lambda a: a.astype(BF16)
    row = lambda a: a.reshape(DEPTH, 1, a.shape[-1])
    return {
        "w_in": bf(w_in_p), "q_norm": row(q_norm), "kv_norm": row(kv_norm),
        "wq": bf(wq.reshape(DEPTH, Q_LORA, N_HEADS * HEAD_SLAB)),
        "wq_sw": bf(wq_sw.reshape(DEPTH, Q_LORA, N_HEADS * HEAD_SLAB)),
        "wk": bf(wk), "wk_sw": bf(wk_sw), "w_uvt": bf(w_uvt), "v_ones": jnp.asarray(v_ones.reshape(VT_ROWS, 1)),
        "sgu_g": row(sgu_ln_g), "sgu_b": row(sgu_ln_b), "ws_cat": bf(ws_cat), "sgu_bias": sgu_bias,
        "wp_bd": bf(wp_bd), "pool_scale": row(pool_scale), "w_f": bf(w_fourier),
        "w_out_a": bf(w_out[:, :ATTN_DIM]), "w_out_b": bf(w_out[:, ATTN_DIM:]),
        "ln1_g": row(ln1_g), "ln1_b": row(ln1_b), "w1": bf(w_ffn1), "w3": bf(w_ffn3), "w2": bf(w_ffn2),
        "ln2_g": row(ln2_g), "ln2_b": row(ln2_b),
    }


def kernel(x, c, ctx, c_ctx, w_mod, b_mod, w_in, q_norm, w_uq, kv_norm, w_uk, w_uv, sgu_ln_g, sgu_ln_b, w_spatial,
           b_spatial, w_pool, pool_scale, w_fourier, w_out, ln1_g, ln1_b, w_ffn1, w_ffn3, w_ffn2, ln2_g, ln2_b):
    B, L, D = x.shape
    Lc = ctx.shape[1]
    assert D == D_MODEL and L % TM == 0 and L % TQ == 0 and L % GRID_W == 0 and L % Lc == 0
    assert TM % Lc == 0 and (B * Lc) % TM == 0 and Lc % CHUNK == 0 and B < MOD_ROWS and L % KEY_CHUNK == 0
    dims = (B, L, Lc)
    r_lat = B * L
    xs = (x.reshape(r_lat, D), ctx.reshape(B * Lc, D))

    cc = jnp.concatenate([c, c_ctx[None, :], jnp.zeros((MOD_ROWS - B - 1, D), F32)], axis=0)
    mod_all = _modulation(cc, w_mod, b_mod).reshape(DEPTH, MOD_ROWS, 1, 6 * D)

    wts = _prep_weights(w_in, q_norm, w_uq, kv_norm, w_uk, w_uv, sgu_ln_g, sgu_ln_b, w_spatial, b_spatial, w_pool,
                        pool_scale, w_fourier, w_out, ln1_g, ln1_b, w_ffn1, w_ffn3, w_ffn2, ln2_g, ln2_b)
    cos_t, sin_t = _rope_tables(L)
    tabs_lat = _seq_tables(L)
    tabs_ctx = _seq_tables(Lc)

    for l in range(DEPTH):
        last = l == DEPTH - 1
        q, k, vt, loc_in, pool_in = _inproj(l, xs, mod_all, cos_t, sin_t, wts, dims)
        attn = _attention_latent(q, k, vt, dims)
        loc = _local_mixers(l, loc_in, pool_in, tabs_lat, wts, L, 0, B)
        ctx_parts = None
        if not last:
            ctx_parts = (_attention_context(q, k, vt, dims),
                         _local_mixers(l, loc_in, pool_in, tabs_ctx, wts, Lc, r_lat // Lc, B))
        xs = (_tail(l, xs, attn, loc, ctx_parts, mod_all, wts, dims),)
    return xs[0].reshape(B, L, D)
```

```python
import functools
import math

import numpy as np
import jax
import jax.numpy as jnp
from jax import lax
from jax.experimental import pallas as pl
from jax.experimental.pallas import tpu as pltpu

F32 = jnp.float32
BF16 = jnp.bfloat16

D_MODEL = 1024
DEPTH = 4
GRID_W = 64
N_HEADS = 8
QK_NOPE = 64
QK_ROPE = 32
V_DIM = 64
Q_LORA = 256
KV_LORA = 128
ROPE_BASE = 10000.0
SGU_DIM = 256
SGU_HEADS = 4
CHUNK = 128
POOL_DIM = 256
POOL_WINDOWS = (2, 4, 8, 16)
POOL_GDIM = POOL_DIM // len(POOL_WINDOWS)
FOURIER_DIM = 256
FOURIER_HEADS = 4
ATTN_DIM = N_HEADS * V_DIM
LOC_DIM = SGU_DIM + POOL_DIM + FOURIER_DIM
LOC_IN = 2 * SGU_DIM + FOURIER_DIM
OFF_SGU = Q_LORA + KV_LORA + QK_ROPE
IN_DIM = OFF_SGU + 2 * SGU_DIM + POOL_DIM + FOURIER_DIM
D_FF = ((8 * D_MODEL + 3 * 256 - 1) // (3 * 256)) * 256
LN_EPS = 1e-6
DEEPNORM_ALPHA = (2.0 * DEPTH) ** 0.25
QK_SCALE_LOG2 = (QK_NOPE + QK_ROPE) ** -0.5 * math.log2(math.e)
SGU_GRP_SHIFT = (SGU_DIM // SGU_HEADS).bit_length() - 1
POOL_GRP_SHIFT = POOL_GDIM.bit_length() - 1

LANE = 128
BF16_SUBLANES = 16
HEAD_SLAB = LANE
V_SLAB = V_DIM + BF16_SUBLANES
VT_ROWS = N_HEADS * V_SLAB
PROJ_A = 512
PROJ_W = PROJ_A + 2 * SGU_DIM + POOL_DIM + FOURIER_DIM
POOL_PAD = 16
VMEM_LIMIT = 56 * 1024 * 1024

TM = 512
TQ = 512
KEY_CHUNK = 512
TAIL_SPLIT = 2
SGU_UNROLL = 4
MOD_NB = 1536
MOD_ROWS = 16


def _layer_block(shape, l):
    n = len(shape)
    return pl.BlockSpec((None,) + tuple(shape), lambda *_: (l,) + (0,) * n, pipeline_mode=pl.Buffered(1))


def _resident(shape):
    n = len(shape)
    return pl.BlockSpec(shape, lambda *_: (0,) * n, pipeline_mode=pl.Buffered(1))


def _dot(a, b):
    return jnp.dot(a, b, preferred_element_type=F32)


def _dot_nt(a, b):
    return lax.dot_general(a, b, (((1,), (1,)), ((), ())), preferred_element_type=F32)


def _layer_norm(x, g, b):
    mu = jnp.mean(x, axis=-1, keepdims=True)
    xc = x - mu
    var = jnp.mean(xc * xc, axis=-1, keepdims=True)
    return xc * lax.rsqrt(var + LN_EPS) * g + b


def _rms_norm(x, g):
    return x * lax.rsqrt(jnp.mean(x * x, axis=-1, keepdims=True) + LN_EPS) * g


def _mod_kernel(c_ref, w_ref, b_ref, o_ref):
    c = c_ref[...]
    s = c * (1.0 / (1.0 + jnp.exp(-c)))
    o_ref[0] = _dot(s.astype(BF16), w_ref[0].astype(BF16)) + b_ref[0]


def _modulation(cc, w_mod, b_mod):
    d6 = w_mod.shape[-1]
    return pl.pallas_call(
        _mod_kernel,
        grid=(DEPTH, d6 // MOD_NB),
        in_specs=[
            pl.BlockSpec((MOD_ROWS, D_MODEL), lambda l, n: (0, 0)),
            pl.BlockSpec((1, D_MODEL, MOD_NB), lambda l, n: (l, 0, n)),
            pl.BlockSpec((1, 1, MOD_NB), lambda l, n: (l, 0, n)),
        ],
        out_specs=pl.BlockSpec((1, MOD_ROWS, MOD_NB), lambda l, n: (l, 0, n)),
        out_shape=jax.ShapeDtypeStruct((DEPTH, MOD_ROWS, d6), F32),
        compiler_params=pltpu.CompilerParams(dimension_semantics=("arbitrary", "arbitrary")),
        name="modulation",
    )(cc, w_mod, b_mod.reshape(DEPTH, 1, d6))


def _mod_spec(l, npb, B):
    return pl.BlockSpec((None, 1, 1, 6 * D_MODEL), lambda j: (l, jnp.minimum(j // npb, B), 0, 0))


def _token_rows(x_refs, nlat):
    if len(x_refs) == 1:
        return x_refs[0][...]
    return jnp.where(pl.program_id(0) < nlat, x_refs[0][...], x_refs[1][...])


def _token_specs(xs, nlat):
    if len(xs) == 1:
        return [pl.BlockSpec((TM, D_MODEL), lambda j: (j, 0))]
    return [pl.BlockSpec((TM, D_MODEL), lambda j: (jnp.minimum(j, nlat - 1), 0)),
            pl.BlockSpec((TM, D_MODEL), lambda j: (jnp.maximum(j - nlat, 0), 0))]


def _inproj_kernel(*refs, nx, nlat):
    (mod_ref, cos_ref, sin_ref, win_ref, qn_ref, kvn_ref, wq_ref, wqs_ref,
     wk_ref, wks_ref, wuvt_ref, vones_ref, q_ref, k_ref, vt_ref, loc_ref, pool_ref) = refs[nx:]
    x = _token_rows(refs[:nx], nlat)
    sh1 = mod_ref[0, :, 0:D_MODEL]
    sc1 = mod_ref[0, :, D_MODEL:2 * D_MODEL]
    h = (x * (1.0 + sc1) + sh1).astype(BF16)
    proj = _dot(h, win_ref[...])
    cqn = _rms_norm(proj[:, 0:Q_LORA], qn_ref[...]).astype(BF16)
    ckvn = _rms_norm(proj[:, Q_LORA:Q_LORA + KV_LORA], kvn_ref[...]).astype(BF16)
    xk = jnp.concatenate([ckvn, proj[:, Q_LORA + KV_LORA:PROJ_A].astype(BF16)], axis=1)
    q = _dot(cqn, wq_ref[...])
    qs = _dot(cqn, wqs_ref[...])
    k = _dot(xk, wk_ref[...])
    ks = _dot(xk, wks_ref[...])
    cos = cos_ref[...]
    sin = sin_ref[...]
    cos_q = cos * QK_SCALE_LOG2
    sin_q = sin * QK_SCALE_LOG2
    for hd in range(N_HEADS):
        sl = slice(hd * HEAD_SLAB, (hd + 1) * HEAD_SLAB)
        q_ref[hd] = (q[:, sl] * cos_q + qs[:, sl] * sin_q).astype(BF16)
        k_ref[hd] = (k[:, sl] * cos + ks[:, sl] * sin).astype(BF16)
    vt_ref[...] = (_dot_nt(wuvt_ref[...], ckvn) + vones_ref[...]).astype(BF16)
    loc_ref[:, 0:2 * SGU_DIM] = proj[:, PROJ_A:PROJ_A + 2 * SGU_DIM].astype(BF16)
    loc_ref[:, 2 * SGU_DIM:] = proj[:, PROJ_W - FOURIER_DIM:].astype(BF16)
    pool_ref[...] = proj[:, PROJ_A + 2 * SGU_DIM:PROJ_A + 2 * SGU_DIM + POOL_DIM]


def _inproj(l, xs, mod_all, cos_t, sin_t, wts, dims):
    B, L, Lc = dims
    R = B * (L + Lc)
    nlat = B * L // TM
    npb = L // TM

    def tab_idx(j):
        return (jnp.where(j < nlat, j % npb, npb), 0)

    return pl.pallas_call(
        functools.partial(_inproj_kernel, nx=len(xs), nlat=nlat),
        grid=(R // TM,),
        in_specs=_token_specs(xs, nlat) + [
            _mod_spec(l, npb, B),
            pl.BlockSpec((TM, HEAD_SLAB), tab_idx),
            pl.BlockSpec((TM, HEAD_SLAB), tab_idx),
            _layer_block((D_MODEL, PROJ_W), l),
            _layer_block((1, Q_LORA), l),
            _layer_block((1, KV_LORA), l),
            _layer_block((Q_LORA, N_HEADS * HEAD_SLAB), l),
            _layer_block((Q_LORA, N_HEADS * HEAD_SLAB), l),
            _layer_block((PROJ_A - Q_LORA, N_HEADS * HEAD_SLAB), l),
            _layer_block((PROJ_A - Q_LORA, N_HEADS * HEAD_SLAB), l),
            _layer_block((VT_ROWS, KV_LORA), l),
            _resident((VT_ROWS, 1)),
        ],
        out_specs=[
            pl.BlockSpec((N_HEADS, TM, HEAD_SLAB), lambda j: (0, j, 0)),
            pl.BlockSpec((N_HEADS, TM, HEAD_SLAB), lambda j: (0, j, 0)),
            pl.BlockSpec((VT_ROWS, TM), lambda j: (0, j)),
            pl.BlockSpec((TM, LOC_IN), lambda j: (j, 0)),
            pl.BlockSpec((TM, POOL_DIM), lambda j: (j, 0)),
        ],
        out_shape=[
            jax.ShapeDtypeStruct((N_HEADS, R, HEAD_SLAB), BF16),
            jax.ShapeDtypeStruct((N_HEADS, R, HEAD_SLAB), BF16),
            jax.ShapeDtypeStruct((VT_ROWS, R), BF16),
            jax.ShapeDtypeStruct((R, LOC_IN), BF16),
            jax.ShapeDtypeStruct((R, POOL_DIM), F32),
        ],
        compiler_params=pltpu.CompilerParams(dimension_semantics=("arbitrary",), vmem_limit_bytes=VMEM_LIMIT),
        name="inproj",
    )(*xs, mod_all, cos_t, sin_t, wts["w_in"], wts["q_norm"], wts["kv_norm"], wts["wq"], wts["wq_sw"],
      wts["wk"], wts["wk_sw"], wts["w_uvt"], wts["v_ones"])


def _attn_kernel(*refs, with_latent, tq, nq, kc):
    if with_latent:
        (q_ref, kl_ref, kc_ref, vl_ref, vc_ref, o_ref, acc_ref, sla_ref, sca_ref, ma_ref, slb_ref, scb_ref,
         mb_ref) = refs
        bufs = ((sla_ref, sca_ref, ma_ref), (slb_ref, scb_ref, mb_ref))
        n_lat = kl_ref.shape[1] // kc
    else:
        q_ref, kc_ref, vc_ref, o_ref, acc_ref, sca_ref, ma_ref, scb_ref, mb_ref = refs
        bufs = ((None, sca_ref, ma_ref), (None, scb_ref, mb_ref))
        n_lat = 0
    head_bits = N_HEADS.bit_length() - 1
    assert N_HEADS == 1 << head_bits and N_HEADS % 2 == 0
    n_stage = nq * N_HEADS

    def stage(st_soft, buf_soft, st_score, buf_score):
        if st_score is not None:
            hd2 = st_score & (N_HEADS - 1)
            q0 = pl.multiple_of((st_score >> head_bits) * tq, tq)
            qh = q_ref[hd2, pl.ds(q0, tq), :]
            m_new = None
        if st_soft is not None:
            hd = st_soft & (N_HEADS - 1)
            r0 = pl.multiple_of(hd * V_SLAB, BF16_SUBLANES)
            m = buf_soft[2][...]
            o = None
        for c in range(n_lat + 1):
            if st_score is not None:
                if c == 0:
                    s_c = _dot_nt(kc_ref[hd2], qh)
                    buf_score[1][...] = s_c
                else:
                    s_c = _dot_nt(kl_ref[hd2, pl.ds((c - 1) * kc, kc), :], qh)
                    buf_score[0][pl.ds((c - 1) * kc, kc), :] = s_c
                m_c = jnp.max(s_c, axis=0, keepdims=True)
                m_new = m_c if m_new is None else jnp.maximum(m_new, m_c)
            if st_soft is not None:
                if c == 0:
                    p_c = jnp.exp2(buf_soft[1][...] - m).astype(BF16)
                    o_c = _dot(vc_ref[pl.ds(r0, V_SLAB), :], p_c)
                else:
                    p_c = jnp.exp2(buf_soft[0][pl.ds((c - 1) * kc, kc), :] - m).astype(BF16)
                    o_c = _dot(vl_ref[pl.ds(r0, V_SLAB), pl.ds((c - 1) * kc, kc)], p_c)
                o = o_c if o is None else o + o_c
        if st_score is not None:
            buf_score[2][...] = m_new
        if st_soft is not None:
            a0 = pl.multiple_of(hd * V_DIM, V_DIM)
            acc_ref[pl.ds(a0, V_DIM), :] = o[0:V_DIM] / o[V_DIM:V_DIM + 1]

    def flush(qb):
        o0 = pl.multiple_of(qb * tq, tq)
        o_ref[pl.ds(o0, tq), :] = acc_ref[...].T.astype(BF16)

    stage(None, None, 0, bufs[0])

    def pair(t, carry):
        s0 = 2 * t
        stage(s0, bufs[0], s0 + 1, bufs[1])
        stage(s0 + 1, bufs[1], s0 + 2, bufs[0])

        @pl.when((s0 + 2) & (N_HEADS - 1) == 0)
        def _():
            flush(s0 >> head_bits)

        return carry

    lax.fori_loop(0, n_stage // 2 - 1, pair, 0)
    stage(n_stage - 2, bufs[0], n_stage - 1, bufs[1])
    stage(n_stage - 1, bufs[1], None, None)
    flush(nq - 1)


def _attention_latent(q, k, vt, dims, flat, kc):
    B, L, Lc = dims
    ctx_blk0 = B * L // Lc
    nq = L // TQ
    qrows = L if flat else TQ
    grid = (B,) if flat else (B, nq)
    qmap = (lambda b: (0, b, 0)) if flat else (lambda b, i: (0, b * nq + i, 0))
    omap = (lambda b: (b, 0)) if flat else (lambda b, i: (b * nq + i, 0))
    return pl.pallas_call(
        functools.partial(_attn_kernel, with_latent=True, tq=TQ, nq=nq if flat else 1, kc=kc),
        grid=grid,
        in_specs=[
            pl.BlockSpec((N_HEADS, qrows, HEAD_SLAB), qmap),
            pl.BlockSpec((N_HEADS, L, HEAD_SLAB), lambda b, *_: (0, b, 0)),
            pl.BlockSpec((N_HEADS, Lc, HEAD_SLAB), lambda b, *_: (0, ctx_blk0 + b, 0)),
            pl.BlockSpec((VT_ROWS, L), lambda b, *_: (0, b)),
            pl.BlockSpec((VT_ROWS, Lc), lambda b, *_: (0, ctx_blk0 + b)),
        ],
        out_specs=pl.BlockSpec((qrows, ATTN_DIM), omap),
        out_shape=jax.ShapeDtypeStruct((B * L, ATTN_DIM), BF16),
        scratch_shapes=[pltpu.VMEM((ATTN_DIM, TQ), F32)]
        + [pltpu.VMEM((L, TQ), F32), pltpu.VMEM((Lc, TQ), F32), pltpu.VMEM((1, TQ), F32)] * 2,
        compiler_params=pltpu.CompilerParams(dimension_semantics=("arbitrary",) * len(grid),
                                             vmem_limit_bytes=VMEM_LIMIT),
        name="attn_latent_%s_kc%d" % ("flat" if flat else "blk", kc),
    )(q, k, k, vt, vt)


def _attn_context_kernel(q_ref, k_ref, vt_ref, o_ref):
    lc = q_ref.shape[1]
    s = lax.dot_general(k_ref[...], q_ref[...], (((2,), (2,)), ((0,), (0,))),
                        preferred_element_type=F32)
    p = jnp.exp2(s - jnp.max(s, axis=1, keepdims=True)).astype(BF16)
    vt = vt_ref[...].reshape(N_HEADS, V_SLAB, lc)
    o = lax.dot_general(vt, p, (((2,), (1,)), ((0,), (0,))), preferred_element_type=F32)
    out = o[:, 0:V_DIM, :] / o[:, V_DIM:V_DIM + 1, :]
    o_ref[...] = out.reshape(ATTN_DIM, lc).T.astype(BF16)


def _attention_context(q, k, vt, dims):
    B, L, Lc = dims
    ctx_blk0 = B * L // Lc
    return pl.pallas_call(
        _attn_context_kernel,
        grid=(B,),
        in_specs=[
            pl.BlockSpec((N_HEADS, Lc, HEAD_SLAB), lambda b: (0, ctx_blk0 + b, 0)),
            pl.BlockSpec((N_HEADS, Lc, HEAD_SLAB), lambda b: (0, ctx_blk0 + b, 0)),
            pl.BlockSpec((VT_ROWS, Lc), lambda b: (0, ctx_blk0 + b)),
        ],
        out_specs=pl.BlockSpec((Lc, ATTN_DIM), lambda b: (b, 0)),
        out_shape=jax.ShapeDtypeStruct((B * Lc, ATTN_DIM), BF16),
        compiler_params=pltpu.CompilerParams(dimension_semantics=("arbitrary",)),
        name="attn_context",
    )(q, k, vt)


def _local_kernel(loc_ref, pool_ref, dft_ref, cbd_ref, sbd_ref, wf_ref, sg_ref, sb_ref, ws_ref, bias_ref,
                  wp_ref, ps_ref, icnt_ref, o_ref, bufa, bufb, *, seq):
    lane_grp = lax.shift_right_logical(lax.broadcasted_iota(jnp.int32, (CHUNK, SGU_DIM), 1), SGU_GRP_SHIFT)
    sg = sg_ref[...]
    sb = sb_ref[...]

    def chunk(ci, carry):
        r0 = pl.multiple_of(ci * CHUNK, CHUNK)
        u = loc_ref[pl.ds(r0, CHUNK), 0:SGU_DIM].astype(F32)
        v = loc_ref[pl.ds(r0, CHUNK), SGU_DIM:2 * SGU_DIM].astype(F32)
        vn = _layer_norm(v, sg, sb)
        stacked = jnp.concatenate(
            [jnp.where(lane_grp == g, vn, 0.0).astype(BF16) for g in range(SGU_HEADS)], axis=0)
        mixed = _dot(ws_ref[...], stacked) + bias_ref[...]
        o_ref[pl.ds(r0, CHUNK), 0:SGU_DIM] = (u * mixed).astype(BF16)
        return carry

    lax.fori_loop(0, seq // CHUNK, chunk, 0, unroll=min(SGU_UNROLL, seq // CHUNK))

    x = pool_ref[...]
    n = seq + POOL_PAD
    half = POOL_PAD // 2
    zpad = jnp.zeros((POOL_PAD, POOL_DIM), F32)
    bufa[0:POOL_PAD, :] = zpad
    bufa[seq + POOL_PAD:seq + 2 * POOL_PAD, :] = zpad
    bufa[POOL_PAD:seq + POOL_PAD, :] = x
    bufb[0:half, :] = zpad[0:half]
    bufb[n + half:n + POOL_PAD, :] = zpad[0:half]
    c2 = bufa[pl.ds(half - 1, n), :] + bufa[pl.ds(half, n), :]
    bufb[pl.ds(half, n), :] = c2
    c4 = bufb[pl.ds(half - 1, n), :] + bufb[pl.ds(half + 1, n), :]
    bufa[pl.ds(half, n), :] = c4
    c8 = bufa[pl.ds(half - 2, n), :] + bufa[pl.ds(half + 2, n), :]
    bufb[pl.ds(half, n), :] = c8
    c16 = bufb[pl.ds(half - 4, n), :] + bufb[pl.ds(half + 4, n), :]
    grp = lax.shift_right_logical(lax.broadcasted_iota(jnp.int32, (seq, POOL_DIM), 1), POOL_GRP_SHIFT)
    inner = slice(half, half + seq)
    wsum = jnp.where(grp == 0, c2[inner], jnp.where(grp == 1, c4[inner], jnp.where(grp == 2, c8[inner], c16[inner])))
    centred = (wsum * icnt_ref[...] - x).astype(BF16)
    o_ref[:, SGU_DIM:SGU_DIM + POOL_DIM] = (_dot(centred, wp_ref[...]) * ps_ref[...]).astype(BF16)

    f = loc_ref[:, 2 * SGU_DIM:]
    g = jnp.concatenate([_dot(f, cbd_ref[...]).astype(BF16), _dot(f, sbd_ref[...]).astype(BF16)], axis=0)
    spec = _dot(dft_ref[...], g)
    o_ref[:, SGU_DIM + POOL_DIM:] = _dot(spec.astype(BF16), wf_ref[...]).astype(BF16)


def _local_mixers(l, loc_in, pool_in, tabs, wts, seq, blk0, nseq):
    return pl.pallas_call(
        functools.partial(_local_kernel, seq=seq),
        grid=(nseq,),
        in_specs=[
            pl.BlockSpec((seq, LOC_IN), lambda b: (blk0 + b, 0)),
            pl.BlockSpec((seq, POOL_DIM), lambda b: (blk0 + b, 0)),
            _resident((seq, 2 * seq)),
            _resident((FOURIER_DIM, FOURIER_DIM)),
            _resident((FOURIER_DIM, FOURIER_DIM)),
            _layer_block((FOURIER_DIM, FOURIER_DIM), l),
            _layer_block((1, SGU_DIM), l),
            _layer_block((1, SGU_DIM), l),
            _layer_block((CHUNK, SGU_HEADS * CHUNK), l),
            _layer_block((CHUNK, SGU_DIM), l),
            _layer_block((POOL_DIM, POOL_DIM), l),
            _layer_block((1, POOL_DIM), l),
            _resident((seq, POOL_DIM)),
        ],
        out_specs=pl.BlockSpec((seq, LOC_DIM), lambda b: (b, 0)),
        out_shape=jax.ShapeDtypeStruct((nseq * seq, LOC_DIM), BF16),
        scratch_shapes=[pltpu.VMEM((seq + 2 * POOL_PAD, POOL_DIM), F32),
                        pltpu.VMEM((seq + 2 * POOL_PAD, POOL_DIM), F32)],
        compiler_params=pltpu.CompilerParams(dimension_semantics=("arbitrary",), vmem_limit_bytes=VMEM_LIMIT),
        name="local_mixers_%d" % seq,
    )(loc_in, pool_in, tabs["dft"], tabs["cbd"], tabs["sbd"], wts["w_f"], wts["sgu_g"], wts["sgu_b"],
      wts["ws_cat"], wts["sgu_bias"], wts["wp_bd"], wts["pool_scale"], tabs["icnt"])


def _tail_kernel(*refs, nx, nlat, with_ctx):
    x_refs = refs[:nx]
    refs = refs[nx:]
    if with_ctx:
        (attn_ref, attnc_ref, loc_ref, locc_ref, mod_ref, wo_ref, l1g_ref, l1b_ref, w1_ref, w3_ref,
         w2_ref, l2g_ref, l2b_ref, o_ref) = refs
    else:
        (attn_ref, loc_ref, mod_ref, wo_ref, l1g_ref, l1b_ref, w1_ref, w3_ref, w2_ref,
         l2g_ref, l2b_ref, o_ref) = refs
        attnc_ref = locc_ref = None
    is_lat = pl.program_id(0) < nlat

    def rows(lat_ref, ctx_ref, rs):
        if ctx_ref is None:
            return lat_ref[rs, :]
        return jnp.where(is_lat, lat_ref[rs, :], ctx_ref[rs, :])

    g1 = mod_ref[0, :, 2 * D_MODEL:3 * D_MODEL]
    sh2 = mod_ref[0, :, 3 * D_MODEL:4 * D_MODEL]
    sc2 = mod_ref[0, :, 4 * D_MODEL:5 * D_MODEL]
    g2 = mod_ref[0, :, 5 * D_MODEL:6 * D_MODEL]
    sub = TM // TAIL_SPLIT
    parts = [pl.ds(p * sub, sub) for p in range(TAIL_SPLIT)]
    ys = []
    for rs in parts:
        attn = rows(attn_ref, attnc_ref, rs)
        loc = rows(loc_ref, locc_ref, rs)
        ys.append(_dot(attn, wo_ref[0:ATTN_DIM, :]) + _dot(loc, wo_ref[ATTN_DIM:, :]))
    x1s, ups = [], []
    for rs, y in zip(parts, ys):
        x = rows(x_refs[0], x_refs[1] if nx == 2 else None, rs)
        x1 = _layer_norm(DEEPNORM_ALPHA * x + g1 * y, l1g_ref[...], l1b_ref[...])
        h = (x1 * (1.0 + sc2) + sh2).astype(BF16)
        x1s.append(x1)
        ups.append((_dot(h, w1_ref[...]), _dot(h, w3_ref[...])))
    ffns = []
    for a, b in ups:
        hid = (a * (1.0 / (1.0 + jnp.exp(-a))) * b).astype(BF16)
        ffns.append(_dot(hid, w2_ref[...]))
    for rs, x1, ffn in zip(parts, x1s, ffns):
        o_ref[rs, :] = _layer_norm(DEEPNORM_ALPHA * x1 + g2 * ffn, l2g_ref[...], l2b_ref[...])


def _tail(l, xs, attn, loc, ctx_parts, mod_all, wts, dims):
    B, L, Lc = dims
    npb = L // TM
    nlat = B * L // TM
    rows = B * L if ctx_parts is None else B * (L + Lc)
    lat_idx = lambda j: (jnp.minimum(j, nlat - 1), 0)
    ctx_idx = lambda j: (jnp.maximum(j - nlat, 0), 0)
    act_specs = [pl.BlockSpec((TM, ATTN_DIM), lat_idx), pl.BlockSpec((TM, LOC_DIM), lat_idx)]
    acts = [attn, loc]
    if ctx_parts is not None:
        act_specs = [act_specs[0], pl.BlockSpec((TM, ATTN_DIM), ctx_idx), act_specs[1],
                     pl.BlockSpec((TM, LOC_DIM), ctx_idx)]
        acts = [attn, ctx_parts[0], loc, ctx_parts[1]]
    return pl.pallas_call(
        functools.partial(_tail_kernel, nx=len(xs), nlat=nlat, with_ctx=ctx_parts is not None),
        grid=(rows // TM,),
        in_specs=_token_specs(xs, nlat) + act_specs + [
            _mod_spec(l, npb, B),
            _layer_block((ATTN_DIM + LOC_DIM, D_MODEL), l),
            _layer_block((1, D_MODEL), l),
            _layer_block((1, D_MODEL), l),
            _layer_block((D_MODEL, D_FF), l),
            _layer_block((D_MODEL, D_FF), l),
            _layer_block((D_FF, D_MODEL), l),
            _layer_block((1, D_MODEL), l),
            _layer_block((1, D_MODEL), l),
        ],
        out_specs=pl.BlockSpec((TM, D_MODEL), lambda j: (j, 0)),
        out_shape=jax.ShapeDtypeStruct((rows, D_MODEL), F32),
        compiler_params=pltpu.CompilerParams(dimension_semantics=("arbitrary",), vmem_limit_bytes=VMEM_LIMIT),
        name="tail",
    )(*xs, *acts, mod_all, wts["w_out"], wts["ln1_g"], wts["ln1_b"], wts["w1"], wts["w3"],
      wts["w2"], wts["ln2_g"], wts["ln2_b"])


def _rope_tables(L):
    pos = jnp.arange(L)
    row = (pos // GRID_W).astype(F32)
    col = (pos % GRID_W).astype(F32)
    n_freq = QK_ROPE // 4
    inv_freq = ROPE_BASE ** (-jnp.arange(n_freq, dtype=F32) / n_freq)
    ar = row[:, None] * inv_freq
    ac = col[:, None] * inv_freq
    ones = jnp.ones((L, QK_NOPE), F32)
    zeros = jnp.zeros((L, QK_NOPE), F32)
    pad1 = jnp.ones((L, HEAD_SLAB - QK_NOPE - QK_ROPE), F32)
    pad0 = jnp.zeros((L, HEAD_SLAB - QK_NOPE - QK_ROPE), F32)
    cos = jnp.concatenate([ones, jnp.cos(ar), jnp.cos(ar), jnp.cos(ac), jnp.cos(ac), pad1], axis=1)
    sin = jnp.concatenate([zeros, -jnp.sin(ar), jnp.sin(ar), -jnp.sin(ac), jnp.sin(ac), pad0], axis=1)
    cos = jnp.concatenate([cos, jnp.ones((TM, HEAD_SLAB), F32)], axis=0)
    sin = jnp.concatenate([sin, jnp.zeros((TM, HEAD_SLAB), F32)], axis=0)
    return cos, sin


def _dft_tables(seq):
    idx = np.arange(seq, dtype=np.int64)
    ang = 2.0 * np.pi * ((idx[:, None] * idx[None, :]) % seq).astype(np.float64) / seq
    dft = np.concatenate([np.cos(ang), -np.sin(ang)], axis=1)
    ch = FOURIER_DIM // FOURIER_HEADS
    cidx = np.arange(ch, dtype=np.int64)
    cang = 2.0 * np.pi * ((cidx[:, None] * cidx[None, :]) % ch).astype(np.float64) / ch
    norm = 1.0 / np.sqrt(float(seq) * ch)
    eye = np.eye(FOURIER_HEADS)
    cbd = np.kron(eye, np.cos(cang)) * norm
    sbd = np.kron(eye, np.sin(cang)) * norm
    as_bf16 = lambda a: jnp.asarray(a.astype(np.float32)).astype(BF16)
    return as_bf16(dft), as_bf16(cbd), as_bf16(sbd)


def _pool_inv_counts(seq):
    t = np.arange(seq)
    cols = []
    for w in POOL_WINDOWS:
        lo = np.clip(t - w // 2, 0, seq)
        hi = np.clip(t + w // 2, 0, seq)
        cols.append(np.repeat((1.0 / (hi - lo).astype(np.float64))[:, None], POOL_GDIM, axis=1))
    return jnp.asarray(np.concatenate(cols, axis=1), dtype=F32)


def _seq_tables(seq):
    dft, cbd, sbd = _dft_tables(seq)
    return {"dft": dft, "cbd": cbd, "sbd": sbd, "icnt": _pool_inv_counts(seq)}


def _prep_weights(w_in, q_norm, w_uq, kv_norm, w_uk, w_uv, sgu_ln_g, sgu_ln_b, w_spatial, b_spatial, w_pool,
                  pool_scale, w_fourier, w_out, ln1_g, ln1_b, w_ffn1, w_ffn3, w_ffn2, ln2_g, ln2_b):
    rope_dim = QK_NOPE + QK_ROPE
    slab_pad = HEAD_SLAB - rope_dim
    w_in_p = jnp.concatenate([w_in[..., :OFF_SGU].astype(BF16), jnp.zeros((DEPTH, D_MODEL, PROJ_A - OFF_SGU), BF16),
                              w_in[..., OFF_SGU:].astype(BF16)], axis=-1)
    e = QK_ROPE // 4
    swap = np.concatenate([np.arange(e, 2 * e), np.arange(0, e), np.arange(3 * e, 4 * e), np.arange(2 * e, 3 * e)])
    wq4 = w_uq.reshape(DEPTH, Q_LORA, N_HEADS, rope_dim)
    wq = jnp.pad(wq4, ((0, 0), (0, 0), (0, 0), (0, slab_pad)))
    wq_sw = jnp.concatenate([jnp.zeros((DEPTH, Q_LORA, N_HEADS, QK_NOPE), F32), wq4[..., QK_NOPE:][..., swap],
                             jnp.zeros((DEPTH, Q_LORA, N_HEADS, slab_pad), F32)], axis=-1)
    krows = PROJ_A - Q_LORA
    wk_nope = jnp.pad(w_uk.reshape(DEPTH, KV_LORA, N_HEADS, QK_NOPE), ((0, 0), (0, 0), (0, 0), (0, HEAD_SLAB - QK_NOPE)))
    place = np.zeros((QK_ROPE, HEAD_SLAB), np.float32)
    place[np.arange(QK_ROPE), QK_NOPE + np.arange(QK_ROPE)] = 1.0
    place_sw = np.zeros((QK_ROPE, HEAD_SLAB), np.float32)
    place_sw[swap, QK_NOPE + np.arange(QK_ROPE)] = 1.0
    place_all = jnp.broadcast_to(jnp.asarray(np.tile(place, (1, N_HEADS)))[None], (DEPTH, QK_ROPE, N_HEADS * HEAD_SLAB))
    place_sw_all = jnp.broadcast_to(jnp.asarray(np.tile(place_sw, (1, N_HEADS)))[None],
                                    (DEPTH, QK_ROPE, N_HEADS * HEAD_SLAB))
    ztail = jnp.zeros((DEPTH, krows - KV_LORA - QK_ROPE, N_HEADS * HEAD_SLAB), F32)
    wk = jnp.concatenate([wk_nope.reshape(DEPTH, KV_LORA, N_HEADS * HEAD_SLAB), place_all, ztail], axis=1)
    wk_sw = jnp.concatenate([jnp.zeros((DEPTH, KV_LORA, N_HEADS * HEAD_SLAB), F32), place_sw_all, ztail], axis=1)
    w_uvt = jnp.pad(jnp.transpose(w_uv, (0, 2, 1)).reshape(DEPTH, N_HEADS, V_DIM, KV_LORA),
                    ((0, 0), (0, 0), (0, V_SLAB - V_DIM), (0, 0))).reshape(DEPTH, VT_ROWS, KV_LORA)
    v_ones = np.zeros((N_HEADS, V_SLAB, 1), np.float32)
    v_ones[:, V_DIM:] = 1.0
    ws_cat = jnp.transpose(w_spatial, (0, 2, 1, 3)).reshape(DEPTH, CHUNK, SGU_HEADS * CHUNK)
    sgu_bias = jnp.repeat(jnp.transpose(b_spatial, (0, 2, 1)), SGU_DIM // SGU_HEADS, axis=2)
    wp_bd = jnp.zeros((DEPTH, POOL_DIM, POOL_DIM), F32)
    for gi in range(len(POOL_WINDOWS)):
        sl = slice(gi * POOL_GDIM, (gi + 1) * POOL_GDIM)
        wp_bd = wp_bd.at[:, sl, sl].set(w_pool[:, gi])
    bf = lambda a: a.astype(BF16)
    row = lambda a: a.reshape(DEPTH, 1, a.shape[-1])
    return {
        "w_in": w_in_p, "q_norm": row(q_norm), "kv_norm": row(kv_norm),
        "wq": bf(wq.reshape(DEPTH, Q_LORA, N_HEADS * HEAD_SLAB)),
        "wq_sw": bf(wq_sw.reshape(DEPTH, Q_LORA, N_HEADS * HEAD_SLAB)),
        "wk": bf(wk), "wk_sw": bf(wk_sw), "w_uvt": bf(w_uvt), "v_ones": jnp.asarray(v_ones.reshape(VT_ROWS, 1)),
        "sgu_g": row(sgu_ln_g), "sgu_b": row(sgu_ln_b), "ws_cat": bf(ws_cat), "sgu_bias": sgu_bias,
        "wp_bd": bf(wp_bd), "pool_scale": row(pool_scale), "w_f": bf(w_fourier),
        "w_out": bf(w_out),
        "ln1_g": row(ln1_g), "ln1_b": row(ln1_b), "w1": bf(w_ffn1), "w3": bf(w_ffn3), "w2": bf(w_ffn2),
        "ln2_g": row(ln2_g), "ln2_b": row(ln2_b),
    }


def kernel(x, c, ctx, c_ctx, w_mod, b_mod, w_in, q_norm, w_uq, kv_norm, w_uk, w_uv, sgu_ln_g, sgu_ln_b, w_spatial,
           b_spatial, w_pool, pool_scale, w_fourier, w_out, ln1_g, ln1_b, w_ffn1, w_ffn3, w_ffn2, ln2_g, ln2_b):
    B, L, D = x.shape
    Lc = ctx.shape[1]
    assert D == D_MODEL and L % TM == 0 and L % TQ == 0 and L % GRID_W == 0 and L % Lc == 0
    assert TM % Lc == 0 and (B * Lc) % TM == 0 and Lc % CHUNK == 0 and B < MOD_ROWS and L % KEY_CHUNK == 0
    dims = (B, L, Lc)
    r_lat = B * L
    xs = (x.reshape(r_lat, D), ctx.reshape(B * Lc, D))

    cc = jnp.concatenate([c, c_ctx[None, :], jnp.zeros((MOD_ROWS - B - 1, D), F32)], axis=0)
    mod_all = _modulation(cc, w_mod, b_mod).reshape(DEPTH, MOD_ROWS, 1, 6 * D)

    wts = _prep_weights(w_in, q_norm, w_uq, kv_norm, w_uk, w_uv, sgu_ln_g, sgu_ln_b, w_spatial, b_spatial, w_pool,
                        pool_scale, w_fourier, w_out, ln1_g, ln1_b, w_ffn1, w_ffn3, w_ffn2, ln2_g, ln2_b)
    cos_t, sin_t = _rope_tables(L)
    tabs_lat = _seq_tables(L)
    tabs_ctx = _seq_tables(Lc)

    for l in range(DEPTH):
        last = l == DEPTH - 1
        q, k, vt, loc_in, pool_in = _inproj(l, xs, mod_all, cos_t, sin_t, wts, dims)
        attn = _attention_latent(q, k, vt, dims, flat=l in (1, 2), kc=L if l in (0, 1) else KEY_CHUNK)
        loc = _local_mixers(l, loc_in, pool_in, tabs_lat, wts, L, 0, B)
        ctx_parts = None
        if not last:
            ctx_parts = (_attention_context(q, k, vt, dims),
                         _local_mixers(l, loc_in, pool_in, tabs_ctx, wts, Lc, r_lat // Lc, B))
        xs = (_tail(l, xs, attn, loc, ctx_parts, mod_all, wts, dims),)
    return xs[0].reshape(B, L, D)
```

```python
import functools
import math

import numpy as np
import jax
import jax.numpy as jnp
from jax import lax
from jax.experimental import pallas as pl
from jax.experimental.pallas import tpu as pltpu

F32 = jnp.float32
BF16 = jnp.bfloat16

D_MODEL = 1024
DEPTH = 4
GRID_W = 64
N_HEADS = 8
QK_NOPE = 64
QK_ROPE = 32
V_DIM = 64
Q_LORA = 256
KV_LORA = 128
ROPE_BASE = 10000.0
SGU_DIM = 256
SGU_HEADS = 4
CHUNK = 128
POOL_DIM = 256
POOL_WINDOWS = (2, 4, 8, 16)
POOL_GDIM = POOL_DIM // len(POOL_WINDOWS)
FOURIER_DIM = 256
FOURIER_HEADS = 4
ATTN_DIM = N_HEADS * V_DIM
LOC_DIM = SGU_DIM + POOL_DIM + FOURIER_DIM
LOC_IN = 2 * SGU_DIM + FOURIER_DIM
OFF_SGU = Q_LORA + KV_LORA + QK_ROPE
IN_DIM = OFF_SGU + 2 * SGU_DIM + POOL_DIM + FOURIER_DIM
D_FF = ((8 * D_MODEL + 3 * 256 - 1) // (3 * 256)) * 256
LN_EPS = 1e-6
DEEPNORM_ALPHA = (2.0 * DEPTH) ** 0.25
QK_SCALE_LOG2 = (QK_NOPE + QK_ROPE) ** -0.5 * math.log2(math.e)
SGU_GRP_SHIFT = (SGU_DIM // SGU_HEADS).bit_length() - 1
POOL_GRP_SHIFT = POOL_GDIM.bit_length() - 1

LANE = 128
BF16_SUBLANES = 16
HEAD_SLAB = LANE
V_SLAB = V_DIM + BF16_SUBLANES
VT_ROWS = N_HEADS * V_SLAB
PROJ_A = 512
PROJ_W = PROJ_A + 2 * SGU_DIM + POOL_DIM + FOURIER_DIM
POOL_PAD = 16
VMEM_LIMIT = 56 * 1024 * 1024

TM = 512
TQ = 512
TAIL_SPLIT = 2
SGU_UNROLL = 4
MOD_NB = 1536
MOD_ROWS = 16


def _layer_block(shape, l):
    n = len(shape)
    return pl.BlockSpec((None,) + tuple(shape), lambda *_: (l,) + (0,) * n, pipeline_mode=pl.Buffered(1))


def _resident(shape):
    n = len(shape)
    return pl.BlockSpec(shape, lambda *_: (0,) * n, pipeline_mode=pl.Buffered(1))


def _dot(a, b):
    return jnp.dot(a, b, preferred_element_type=F32)


def _dot_nt(a, b):
    return lax.dot_general(a, b, (((1,), (1,)), ((), ())), preferred_element_type=F32)


def _layer_norm(x, g, b):
    mu = jnp.mean(x, axis=-1, keepdims=True)
    xc = x - mu
    var = jnp.mean(xc * xc, axis=-1, keepdims=True)
    return xc * lax.rsqrt(var + LN_EPS) * g + b


def _rms_norm(x, g):
    return x * lax.rsqrt(jnp.mean(x * x, axis=-1, keepdims=True) + LN_EPS) * g


def _mod_kernel(c_ref, w_ref, b_ref, o_ref):
    c = c_ref[...]
    s = c * (1.0 / (1.0 + jnp.exp(-c)))
    o_ref[0] = _dot(s.astype(BF16), w_ref[0].astype(BF16)) + b_ref[0]


def _modulation(cc, w_mod, b_mod):
    d6 = w_mod.shape[-1]
    return pl.pallas_call(
        _mod_kernel,
        grid=(DEPTH, d6 // MOD_NB),
        in_specs=[
            pl.BlockSpec((MOD_ROWS, D_MODEL), lambda l, n: (0, 0)),
            pl.BlockSpec((1, D_MODEL, MOD_NB), lambda l, n: (l, 0, n)),
            pl.BlockSpec((1, 1, MOD_NB), lambda l, n: (l, 0, n)),
        ],
        out_specs=pl.BlockSpec((1, MOD_ROWS, MOD_NB), lambda l, n: (l, 0, n)),
        out_shape=jax.ShapeDtypeStruct((DEPTH, MOD_ROWS, d6), F32),
        compiler_params=pltpu.CompilerParams(dimension_semantics=("arbitrary", "arbitrary")),
        name="modulation",
    )(cc, w_mod, b_mod.reshape(DEPTH, 1, d6))


def _mod_spec(l, npb, B):
    return pl.BlockSpec((None, 1, 1, 6 * D_MODEL), lambda j: (l, jnp.minimum(j // npb, B), 0, 0))


def _token_rows(x_refs, nlat):
    if len(x_refs) == 1:
        return x_refs[0][...]
    return jnp.where(pl.program_id(0) < nlat, x_refs[0][...], x_refs[1][...])


def _token_specs(xs, nlat):
    if len(xs) == 1:
        return [pl.BlockSpec((TM, D_MODEL), lambda j: (j, 0))]
    return [pl.BlockSpec((TM, D_MODEL), lambda j: (jnp.minimum(j, nlat - 1), 0)),
            pl.BlockSpec((TM, D_MODEL), lambda j: (jnp.maximum(j - nlat, 0), 0))]


def _inproj_kernel(*refs, nx, nlat):
    (mod_ref, qtab_ref, cos_ref, sin_ref, win_ref, qn_ref, kvn_ref, wq_ref,
     wk_ref, wks_ref, wuvt_ref, vones_ref, q_ref, k_ref, vt_ref, loc_ref, pool_ref) = refs[nx:]
    x = _token_rows(refs[:nx], nlat)
    sh1 = mod_ref[0, :, 0:D_MODEL]
    sc1 = mod_ref[0, :, D_MODEL:2 * D_MODEL]
    h = (x * (1.0 + sc1) + sh1).astype(BF16)
    proj = _dot(h, win_ref[...])
    cqn = _rms_norm(proj[:, 0:Q_LORA], qn_ref[...]).astype(BF16)
    ckvn = _rms_norm(proj[:, Q_LORA:Q_LORA + KV_LORA], kvn_ref[...]).astype(BF16)
    xk = jnp.concatenate([ckvn, proj[:, Q_LORA + KV_LORA:PROJ_A].astype(BF16)], axis=1)
    q = _dot(cqn, wq_ref[...])
    k = _dot(xk, wk_ref[...])
    ks = _dot(xk, wks_ref[...])
    cos = cos_ref[...]
    sin = sin_ref[...]
    qtab = qtab_ref[...] * QK_SCALE_LOG2
    for hd in range(N_HEADS):
        sl = slice(hd * HEAD_SLAB, (hd + 1) * HEAD_SLAB)
        q_ref[hd] = (q[:, sl] * qtab).astype(BF16)
        k_ref[hd] = (k[:, sl] * cos + ks[:, sl] * sin).astype(BF16)
    vt_ref[...] = (_dot_nt(wuvt_ref[...], ckvn) + vones_ref[...]).astype(BF16)
    loc_ref[:, 0:2 * SGU_DIM] = proj[:, PROJ_A:PROJ_A + 2 * SGU_DIM].astype(BF16)
    loc_ref[:, 2 * SGU_DIM:] = proj[:, PROJ_W - FOURIER_DIM:].astype(BF16)
    pool_ref[...] = proj[:, PROJ_A + 2 * SGU_DIM:PROJ_A + 2 * SGU_DIM + POOL_DIM]


def _inproj(l, xs, mod_all, rope_tabs, wts, dims):
    B, L, Lc = dims
    R = B * (L + Lc)
    nlat = B * L // TM
    npb = L // TM

    def tab_idx(j):
        return (jnp.where(j < nlat, j % npb, npb), 0)

    return pl.pallas_call(
        functools.partial(_inproj_kernel, nx=len(xs), nlat=nlat),
        grid=(R // TM,),
        in_specs=_token_specs(xs, nlat) + [
            _mod_spec(l, npb, B),
            pl.BlockSpec((TM, HEAD_SLAB), tab_idx),
            pl.BlockSpec((TM, HEAD_SLAB), tab_idx),
            pl.BlockSpec((TM, HEAD_SLAB), tab_idx),
            _layer_block((D_MODEL, PROJ_W), l),
            _layer_block((1, Q_LORA), l),
            _layer_block((1, KV_LORA), l),
            _layer_block((Q_LORA, N_HEADS * HEAD_SLAB), l),
            _layer_block((PROJ_A - Q_LORA, N_HEADS * HEAD_SLAB), l),
            _layer_block((PROJ_A - Q_LORA, N_HEADS * HEAD_SLAB), l),
            _layer_block((VT_ROWS, KV_LORA), l),
            _resident((VT_ROWS, 1)),
        ],
        out_specs=[
            pl.BlockSpec((N_HEADS, TM, HEAD_SLAB), lambda j: (0, j, 0)),
            pl.BlockSpec((N_HEADS, TM, HEAD_SLAB), lambda j: (0, j, 0)),
            pl.BlockSpec((VT_ROWS, TM), lambda j: (0, j)),
            pl.BlockSpec((TM, LOC_IN), lambda j: (j, 0)),
            pl.BlockSpec((TM, POOL_DIM), lambda j: (j, 0)),
        ],
        out_shape=[
            jax.ShapeDtypeStruct((N_HEADS, R, HEAD_SLAB), BF16),
            jax.ShapeDtypeStruct((N_HEADS, R, HEAD_SLAB), BF16),
            jax.ShapeDtypeStruct((VT_ROWS, R), BF16),
            jax.ShapeDtypeStruct((R, LOC_IN), BF16),
            jax.ShapeDtypeStruct((R, POOL_DIM), F32),
        ],
        compiler_params=pltpu.CompilerParams(dimension_semantics=("arbitrary",), vmem_limit_bytes=VMEM_LIMIT),
        name="inproj",
    )(*xs, mod_all, *rope_tabs, wts["w_in"], wts["q_norm"], wts["kv_norm"], wts["wq"],
      wts["wk"], wts["wk_sw"], wts["w_uvt"], wts["v_ones"])


def _attn_kernel(q_ref, kl_ref, kc_ref, vl_ref, vc_ref, o_ref, acc_ref, sla_ref, sca_ref, ma_ref, slb_ref,
                 scb_ref, mb_ref, *, tq, nq):
    bufs = ((sla_ref, sca_ref, ma_ref), (slb_ref, scb_ref, mb_ref))
    head_bits = N_HEADS.bit_length() - 1
    assert N_HEADS == 1 << head_bits and N_HEADS % 2 == 0
    n_stage = nq * N_HEADS

    def stage(st_soft, buf_soft, st_score, buf_score):
        if st_score is not None:
            hd2 = st_score & (N_HEADS - 1)
            q0 = pl.multiple_of((st_score >> head_bits) * tq, tq)
            qh = q_ref[hd2, pl.ds(q0, tq), :]
            s_c = _dot_nt(kc_ref[hd2], qh)
            buf_score[1][...] = s_c
            m_new = jnp.max(s_c, axis=0, keepdims=True)
        if st_soft is not None:
            hd = st_soft & (N_HEADS - 1)
            r0 = pl.multiple_of(hd * V_SLAB, BF16_SUBLANES)
            m = buf_soft[2][...]
            o = _dot(vc_ref[pl.ds(r0, V_SLAB), :], jnp.exp2(buf_soft[1][...] - m).astype(BF16))
        if st_score is not None:
            s_l = _dot_nt(kl_ref[hd2], qh)
            buf_score[0][...] = s_l
            buf_score[2][...] = jnp.maximum(m_new, jnp.max(s_l, axis=0, keepdims=True))
        if st_soft is not None:
            o = o + _dot(vl_ref[pl.ds(r0, V_SLAB), :], jnp.exp2(buf_soft[0][...] - m).astype(BF16))
            a0 = pl.multiple_of(hd * V_DIM, V_DIM)
            acc_ref[pl.ds(a0, V_DIM), :] = o[0:V_DIM] / o[V_DIM:V_DIM + 1]

    def flush(qb):
        o0 = pl.multiple_of(qb * tq, tq)
        o_ref[pl.ds(o0, tq), :] = acc_ref[...].T.astype(BF16)

    stage(None, None, 0, bufs[0])

    def pair(t, carry):
        s0 = 2 * t
        stage(s0, bufs[0], s0 + 1, bufs[1])
        stage(s0 + 1, bufs[1], s0 + 2, bufs[0])

        @pl.when((s0 + 2) & (N_HEADS - 1) == 0)
        def _():
            flush(s0 >> head_bits)

        return carry

    lax.fori_loop(0, n_stage // 2 - 1, pair, 0)
    stage(n_stage - 2, bufs[0], n_stage - 1, bufs[1])
    stage(n_stage - 1, bufs[1], None, None)
    flush(nq - 1)


def _attention_latent(q, k, vt, dims):
    B, L, Lc = dims
    ctx_blk0 = B * L // Lc
    return pl.pallas_call(
        functools.partial(_attn_kernel, tq=TQ, nq=L // TQ),
        grid=(B,),
        in_specs=[
            pl.BlockSpec((N_HEADS, L, HEAD_SLAB), lambda b: (0, b, 0)),
            pl.BlockSpec((N_HEADS, L, HEAD_SLAB), lambda b: (0, b, 0)),
            pl.BlockSpec((N_HEADS, Lc, HEAD_SLAB), lambda b: (0, ctx_blk0 + b, 0)),
            pl.BlockSpec((VT_ROWS, L), lambda b: (0, b)),
            pl.BlockSpec((VT_ROWS, Lc), lambda b: (0, ctx_blk0 + b)),
        ],
        out_specs=pl.BlockSpec((L, ATTN_DIM), lambda b: (b, 0)),
        out_shape=jax.ShapeDtypeStruct((B * L, ATTN_DIM), BF16),
        scratch_shapes=[pltpu.VMEM((ATTN_DIM, TQ), F32)]
        + [pltpu.VMEM((L, TQ), F32), pltpu.VMEM((Lc, TQ), F32), pltpu.VMEM((1, TQ), F32)] * 2,
        compiler_params=pltpu.CompilerParams(dimension_semantics=("arbitrary",), vmem_limit_bytes=VMEM_LIMIT),
        name="attn_latent",
    )(q, k, k, vt, vt)


def _attn_context_kernel(q_ref, k_ref, vt_ref, o_ref):
    lc = q_ref.shape[1]
    s = lax.dot_general(k_ref[...], q_ref[...], (((2,), (2,)), ((0,), (0,))),
                        preferred_element_type=F32)
    p = jnp.exp2(s - jnp.max(s, axis=1, keepdims=True)).astype(BF16)
    vt = vt_ref[...].reshape(N_HEADS, V_SLAB, lc)
    o = lax.dot_general(vt, p, (((2,), (1,)), ((0,), (0,))), preferred_element_type=F32)
    out = o[:, 0:V_DIM, :] / o[:, V_DIM:V_DIM + 1, :]
    o_ref[...] = out.reshape(ATTN_DIM, lc).T.astype(BF16)


def _attention_context(q, k, vt, dims):
    B, L, Lc = dims
    ctx_blk0 = B * L // Lc
    return pl.pallas_call(
        _attn_context_kernel,
        grid=(B,),
        in_specs=[
            pl.BlockSpec((N_HEADS, Lc, HEAD_SLAB), lambda b: (0, ctx_blk0 + b, 0)),
            pl.BlockSpec((N_HEADS, Lc, HEAD_SLAB), lambda b: (0, ctx_blk0 + b, 0)),
            pl.BlockSpec((VT_ROWS, Lc), lambda b: (0, ctx_blk0 + b)),
        ],
        out_specs=pl.BlockSpec((Lc, ATTN_DIM), lambda b: (b, 0)),
        out_shape=jax.ShapeDtypeStruct((B * Lc, ATTN_DIM), BF16),
        compiler_params=pltpu.CompilerParams(dimension_semantics=("arbitrary",)),
        name="attn_context",
    )(q, k, vt)


def _local_kernel(loc_ref, pool_ref, dft_ref, cbd_ref, sbd_ref, wf_ref, sg_ref, sb_ref, ws_ref, bias_ref,
                  wp_ref, ps_ref, icnt_ref, o_ref, bufa, bufb, *, seq):
    lane_grp = lax.shift_right_logical(lax.broadcasted_iota(jnp.int32, (CHUNK, SGU_DIM), 1), SGU_GRP_SHIFT)
    sg = sg_ref[...]
    sb = sb_ref[...]

    def chunk(ci, carry):
        r0 = pl.multiple_of(ci * CHUNK, CHUNK)
        u = loc_ref[pl.ds(r0, CHUNK), 0:SGU_DIM].astype(F32)
        v = loc_ref[pl.ds(r0, CHUNK), SGU_DIM:2 * SGU_DIM].astype(F32)
        vn = _layer_norm(v, sg, sb)
        stacked = jnp.concatenate(
            [jnp.where(lane_grp == g, vn, 0.0).astype(BF16) for g in range(SGU_HEADS)], axis=0)
        mixed = _dot(ws_ref[...], stacked) + bias_ref[...]
        o_ref[pl.ds(r0, CHUNK), 0:SGU_DIM] = (u * mixed).astype(BF16)
        return carry

    lax.fori_loop(0, seq // CHUNK, chunk, 0, unroll=min(SGU_UNROLL, seq // CHUNK))

    x = pool_ref[...]
    n = seq + POOL_PAD
    half = POOL_PAD // 2
    zpad = jnp.zeros((POOL_PAD, POOL_DIM), F32)
    bufa[0:POOL_PAD, :] = zpad
    bufa[seq + POOL_PAD:seq + 2 * POOL_PAD, :] = zpad
    bufa[POOL_PAD:seq + POOL_PAD, :] = x
    bufb[0:half, :] = zpad[0:half]
    bufb[n + half:n + POOL_PAD, :] = zpad[0:half]
    c2 = bufa[pl.ds(half - 1, n), :] + bufa[pl.ds(half, n), :]
    bufb[pl.ds(half, n), :] = c2
    c4 = bufb[pl.ds(half - 1, n), :] + bufb[pl.ds(half + 1, n), :]
    bufa[pl.ds(half, n), :] = c4
    c8 = bufa[pl.ds(half - 2, n), :] + bufa[pl.ds(half + 2, n), :]
    bufb[pl.ds(half, n), :] = c8
    c16 = bufb[pl.ds(half - 4, n), :] + bufb[pl.ds(half + 4, n), :]
    grp = lax.shift_right_logical(lax.broadcasted_iota(jnp.int32, (seq, POOL_DIM), 1), POOL_GRP_SHIFT)
    inner = slice(half, half + seq)
    wsum = jnp.where(grp == 0, c2[inner], jnp.where(grp == 1, c4[inner], jnp.where(grp == 2, c8[inner], c16[inner])))
    centred = (wsum * icnt_ref[...] - x).astype(BF16)
    o_ref[:, SGU_DIM:SGU_DIM + POOL_DIM] = (_dot(centred, wp_ref[...]) * ps_ref[...]).astype(BF16)

    f = loc_ref[:, 2 * SGU_DIM:]
    g = jnp.concatenate([_dot(f, cbd_ref[...]).astype(BF16), _dot(f, sbd_ref[...]).astype(BF16)], axis=0)
    spec = _dot(dft_ref[...], g)
    o_ref[:, SGU_DIM + POOL_DIM:] = _dot(spec.astype(BF16), wf_ref[...]).astype(BF16)


def _local_mixers(l, loc_in, pool_in, tabs, wts, seq, blk0, nseq):
    return pl.pallas_call(
        functools.partial(_local_kernel, seq=seq),
        grid=(nseq,),
        in_specs=[
            pl.BlockSpec((seq, LOC_IN), lambda b: (blk0 + b, 0)),
            pl.BlockSpec((seq, POOL_DIM), lambda b: (blk0 + b, 0)),
            _resident((seq, 2 * seq)),
            _resident((FOURIER_DIM, FOURIER_DIM)),
            _resident((FOURIER_DIM, FOURIER_DIM)),
            _layer_block((FOURIER_DIM, FOURIER_DIM), l),
            _layer_block((1, SGU_DIM), l),
            _layer_block((1, SGU_DIM), l),
            _layer_block((CHUNK, SGU_HEADS * CHUNK), l),
            _layer_block((CHUNK, SGU_DIM), l),
            _layer_block((POOL_DIM, POOL_DIM), l),
            _layer_block((1, POOL_DIM), l),
            _resident((seq, POOL_DIM)),
        ],
        out_specs=pl.BlockSpec((seq, LOC_DIM), lambda b: (b, 0)),
        out_shape=jax.ShapeDtypeStruct((nseq * seq, LOC_DIM), BF16),
        scratch_shapes=[pltpu.VMEM((seq + 2 * POOL_PAD, POOL_DIM), F32),
                        pltpu.VMEM((seq + 2 * POOL_PAD, POOL_DIM), F32)],
        compiler_params=pltpu.CompilerParams(dimension_semantics=("arbitrary",), vmem_limit_bytes=VMEM_LIMIT),
        name="local_mixers_%d" % seq,
    )(loc_in, pool_in, tabs["dft"], tabs["cbd"], tabs["sbd"], wts["w_f"], wts["sgu_g"], wts["sgu_b"],
      wts["ws_cat"], wts["sgu_bias"], wts["wp_bd"], wts["pool_scale"], tabs["icnt"])


def _tail_kernel(*refs, nx, nlat, with_ctx):
    x_refs = refs[:nx]
    refs = refs[nx:]
    if with_ctx:
        (attn_ref, attnc_ref, loc_ref, locc_ref, mod_ref, wo_ref, l1g_ref, l1b_ref, w1_ref, w3_ref,
         w2_ref, l2g_ref, l2b_ref, o_ref) = refs
    else:
        (attn_ref, loc_ref, mod_ref, wo_ref, l1g_ref, l1b_ref, w1_ref, w3_ref, w2_ref,
         l2g_ref, l2b_ref, o_ref) = refs
        attnc_ref = locc_ref = None
    is_lat = pl.program_id(0) < nlat

    def rows(lat_ref, ctx_ref, rs):
        if ctx_ref is None:
            return lat_ref[rs, :]
        return jnp.where(is_lat, lat_ref[rs, :], ctx_ref[rs, :])

    g1 = mod_ref[0, :, 2 * D_MODEL:3 * D_MODEL]
    sh2 = mod_ref[0, :, 3 * D_MODEL:4 * D_MODEL]
    sc2 = mod_ref[0, :, 4 * D_MODEL:5 * D_MODEL]
    g2 = mod_ref[0, :, 5 * D_MODEL:6 * D_MODEL]
    sub = TM // TAIL_SPLIT
    parts = [pl.ds(p * sub, sub) for p in range(TAIL_SPLIT)]
    ys = []
    for rs in parts:
        attn = rows(attn_ref, attnc_ref, rs)
        loc = rows(loc_ref, locc_ref, rs)
        ys.append(_dot(attn, wo_ref[0:ATTN_DIM, :]) + _dot(loc, wo_ref[ATTN_DIM:, :]))
    x1s, ups = [], []
    for rs, y in zip(parts, ys):
        x = rows(x_refs[0], x_refs[1] if nx == 2 else None, rs)
        x1 = _layer_norm(DEEPNORM_ALPHA * x + g1 * y, l1g_ref[...], l1b_ref[...])
        h = (x1 * (1.0 + sc2) + sh2).astype(BF16)
        x1s.append(x1)
        ups.append((_dot(h, w1_ref[...]), _dot(h, w3_ref[...])))
    ffns = []
    for a, b in ups:
        hid = (a * (1.0 / (1.0 + jnp.exp(-a))) * b).astype(BF16)
        ffns.append(_dot(hid, w2_ref[...]))
    for rs, x1, ffn in zip(parts, x1s, ffns):
        o_ref[rs, :] = _layer_norm(DEEPNORM_ALPHA * x1 + g2 * ffn, l2g_ref[...], l2b_ref[...])


def _tail(l, xs, attn, loc, ctx_parts, mod_all, wts, dims):
    B, L, Lc = dims
    npb = L // TM
    nlat = B * L // TM
    rows = B * L if ctx_parts is None else B * (L + Lc)
    lat_idx = lambda j: (jnp.minimum(j, nlat - 1), 0)
    ctx_idx = lambda j: (jnp.maximum(j - nlat, 0), 0)
    act_specs = [pl.BlockSpec((TM, ATTN_DIM), lat_idx), pl.BlockSpec((TM, LOC_DIM), lat_idx)]
    acts = [attn, loc]
    if ctx_parts is not None:
        act_specs = [act_specs[0], pl.BlockSpec((TM, ATTN_DIM), ctx_idx), act_specs[1],
                     pl.BlockSpec((TM, LOC_DIM), ctx_idx)]
        acts = [attn, ctx_parts[0], loc, ctx_parts[1]]
    return pl.pallas_call(
        functools.partial(_tail_kernel, nx=len(xs), nlat=nlat, with_ctx=ctx_parts is not None),
        grid=(rows // TM,),
        in_specs=_token_specs(xs, nlat) + act_specs + [
            _mod_spec(l, npb, B),
            _layer_block((ATTN_DIM + LOC_DIM, D_MODEL), l),
            _layer_block((1, D_MODEL), l),
            _layer_block((1, D_MODEL), l),
            _layer_block((D_MODEL, D_FF), l),
            _layer_block((D_MODEL, D_FF), l),
            _layer_block((D_FF, D_MODEL), l),
            _layer_block((1, D_MODEL), l),
            _layer_block((1, D_MODEL), l),
        ],
        out_specs=pl.BlockSpec((TM, D_MODEL), lambda j: (j, 0)),
        out_shape=jax.ShapeDtypeStruct((rows, D_MODEL), F32),
        compiler_params=pltpu.CompilerParams(dimension_semantics=("arbitrary",), vmem_limit_bytes=VMEM_LIMIT),
        name="tail",
    )(*xs, *acts, mod_all, wts["w_out"], wts["ln1_g"], wts["ln1_b"], wts["w1"], wts["w3"],
      wts["w2"], wts["ln2_g"], wts["ln2_b"])


def _rope_tables(L):
    assert HEAD_SLAB == QK_NOPE + 2 * QK_ROPE
    pos = jnp.arange(L)
    row = (pos // GRID_W).astype(F32)
    col = (pos % GRID_W).astype(F32)
    n_freq = QK_ROPE // 4
    inv_freq = ROPE_BASE ** (-jnp.arange(n_freq, dtype=F32) / n_freq)
    ar = row[:, None] * inv_freq
    ac = col[:, None] * inv_freq
    ones = jnp.ones((L, QK_NOPE), F32)
    zeros = jnp.zeros((L, QK_NOPE), F32)
    cos = jnp.concatenate([jnp.cos(ar), jnp.cos(ar), jnp.cos(ac), jnp.cos(ac)], axis=1)
    sin = jnp.concatenate([-jnp.sin(ar), jnp.sin(ar), -jnp.sin(ac), jnp.sin(ac)], axis=1)
    ident = jnp.concatenate([jnp.ones((TM, QK_NOPE + QK_ROPE), F32), jnp.zeros((TM, QK_ROPE), F32)], axis=1)
    qtab = jnp.concatenate([jnp.concatenate([ones, cos, sin], axis=1), ident], axis=0)
    kcos = jnp.concatenate([jnp.concatenate([ones, cos, cos], axis=1), jnp.ones((TM, HEAD_SLAB), F32)], axis=0)
    ksin = jnp.concatenate([jnp.concatenate([zeros, sin, sin], axis=1), jnp.zeros((TM, HEAD_SLAB), F32)], axis=0)
    return qtab, kcos, ksin


def _dft_tables(seq):
    idx = np.arange(seq, dtype=np.int64)
    ang = 2.0 * np.pi * ((idx[:, None] * idx[None, :]) % seq).astype(np.float64) / seq
    dft = np.concatenate([np.cos(ang), -np.sin(ang)], axis=1)
    ch = FOURIER_DIM // FOURIER_HEADS
    cidx = np.arange(ch, dtype=np.int64)
    cang = 2.0 * np.pi * ((cidx[:, None] * cidx[None, :]) % ch).astype(np.float64) / ch
    norm = 1.0 / np.sqrt(float(seq) * ch)
    eye = np.eye(FOURIER_HEADS)
    cbd = np.kron(eye, np.cos(cang)) * norm
    sbd = np.kron(eye, np.sin(cang)) * norm
    as_bf16 = lambda a: jnp.asarray(a.astype(np.float32)).astype(BF16)
    return as_bf16(dft), as_bf16(cbd), as_bf16(sbd)


def _pool_inv_counts(seq):
    t = np.arange(seq)
    cols = []
    for w in POOL_WINDOWS:
        lo = np.clip(t - w // 2, 0, seq)
        hi = np.clip(t + w // 2, 0, seq)
        cols.append(np.repeat((1.0 / (hi - lo).astype(np.float64))[:, None], POOL_GDIM, axis=1))
    return jnp.asarray(np.concatenate(cols, axis=1), dtype=F32)


def _seq_tables(seq):
    dft, cbd, sbd = _dft_tables(seq)
    return {"dft": dft, "cbd": cbd, "sbd": sbd, "icnt": _pool_inv_counts(seq)}


def _prep_weights(w_in, q_norm, w_uq, kv_norm, w_uk, w_uv, sgu_ln_g, sgu_ln_b, w_spatial, b_spatial, w_pool,
                  pool_scale, w_fourier, w_out, ln1_g, ln1_b, w_ffn1, w_ffn3, w_ffn2, ln2_g, ln2_b):
    rope_dim = QK_NOPE + QK_ROPE
    w_in16 = w_in.astype(BF16)
    w_in_p = jnp.concatenate([w_in16[..., :OFF_SGU], jnp.zeros((DEPTH, D_MODEL, PROJ_A - OFF_SGU), BF16),
                              w_in16[..., OFF_SGU:]], axis=-1)
    e = QK_ROPE // 4
    swap = np.concatenate([np.arange(e, 2 * e), np.arange(0, e), np.arange(3 * e, 4 * e), np.arange(2 * e, 3 * e)])
    wq4 = w_uq.reshape(DEPTH, Q_LORA, N_HEADS, rope_dim)
    wq = jnp.concatenate([wq4, wq4[..., QK_NOPE:][..., swap]], axis=-1)
    krows = PROJ_A - Q_LORA
    wk_nope = jnp.pad(w_uk.reshape(DEPTH, KV_LORA, N_HEADS, QK_NOPE), ((0, 0), (0, 0), (0, 0), (0, HEAD_SLAB - QK_NOPE)))
    place = np.zeros((QK_ROPE, HEAD_SLAB), np.float32)
    place_sw = np.zeros((QK_ROPE, HEAD_SLAB), np.float32)
    for seg in (QK_NOPE, QK_NOPE + QK_ROPE):
        place[np.arange(QK_ROPE), seg + np.arange(QK_ROPE)] = 1.0
        place_sw[swap, seg + np.arange(QK_ROPE)] = 1.0
    place_all = jnp.broadcast_to(jnp.asarray(np.tile(place, (1, N_HEADS)))[None], (DEPTH, QK_ROPE, N_HEADS * HEAD_SLAB))
    place_sw_all = jnp.broadcast_to(jnp.asarray(np.tile(place_sw, (1, N_HEADS)))[None],
                                    (DEPTH, QK_ROPE, N_HEADS * HEAD_SLAB))
    ztail = jnp.zeros((DEPTH, krows - KV_LORA - QK_ROPE, N_HEADS * HEAD_SLAB), F32)
    wk = jnp.concatenate([wk_nope.reshape(DEPTH, KV_LORA, N_HEADS * HEAD_SLAB), place_all, ztail], axis=1)
    wk_sw = jnp.concatenate([jnp.zeros((DEPTH, KV_LORA, N_HEADS * HEAD_SLAB), F32), place_sw_all, ztail], axis=1)
    w_uvt = jnp.pad(jnp.transpose(w_uv, (0, 2, 1)).reshape(DEPTH, N_HEADS, V_DIM, KV_LORA),
                    ((0, 0), (0, 0), (0, V_SLAB - V_DIM), (0, 0))).reshape(DEPTH, VT_ROWS, KV_LORA)
    v_ones = np.zeros((N_HEADS, V_SLAB, 1), np.float32)
    v_ones[:, V_DIM:] = 1.0
    ws_cat = jnp.transpose(w_spatial, (0, 2, 1, 3)).reshape(DEPTH, CHUNK, SGU_HEADS * CHUNK)
    sgu_bias = jnp.repeat(jnp.transpose(b_spatial, (0, 2, 1)), SGU_DIM // SGU_HEADS, axis=2)
    wp_bd = jnp.zeros((DEPTH, POOL_DIM, POOL_DIM), F32)
    for gi in range(len(POOL_WINDOWS)):
        sl = slice(gi * POOL_GDIM, (gi + 1) * POOL_GDIM)
        wp_bd = wp_bd.at[:, sl, sl].set(w_pool[:, gi])
    bf = lambda a: a.astype(BF16)
    row = lambda a: a.reshape(DEPTH, 1, a.shape[-1])
    return {
        "w_in": w_in_p, "q_norm": row(q_norm), "kv_norm": row(kv_norm),
        "wq": bf(wq.reshape(DEPTH, Q_LORA, N_HEADS * HEAD_SLAB)),
        "wk": bf(wk), "wk_sw": bf(wk_sw), "w_uvt": bf(w_uvt), "v_ones": jnp.asarray(v_ones.reshape(VT_ROWS, 1)),
        "sgu_g": row(sgu_ln_g), "sgu_b": row(sgu_ln_b), "ws_cat": bf(ws_cat), "sgu_bias": sgu_bias,
        "wp_bd": bf(wp_bd), "pool_scale": row(pool_scale), "w_f": bf(w_fourier),
        "w_out": bf(w_out),
        "ln1_g": row(ln1_g), "ln1_b": row(ln1_b), "w1": bf(w_ffn1), "w3": bf(w_ffn3), "w2": bf(w_ffn2),
        "ln2_g": row(ln2_g), "ln2_b": row(ln2_b),
    }


def kernel(x, c, ctx, c_ctx, w_mod, b_mod, w_in, q_norm, w_uq, kv_norm, w_uk, w_uv, sgu_ln_g, sgu_ln_b, w_spatial,
           b_spatial, w_pool, pool_scale, w_fourier, w_out, ln1_g, ln1_b, w_ffn1, w_ffn3, w_ffn2, ln2_g, ln2_b):
    B, L, D = x.shape
    Lc = ctx.shape[1]
    assert D == D_MODEL and L % TM == 0 and L % TQ == 0 and L % GRID_W == 0 and L % Lc == 0
    assert TM % Lc == 0 and (B * Lc) % TM == 0 and Lc % CHUNK == 0 and B < MOD_ROWS
    dims = (B, L, Lc)
    r_lat = B * L
    xs = (x.reshape(r_lat, D), ctx.reshape(B * Lc, D))

    cc = jnp.concatenate([c, c_ctx[None, :], jnp.zeros((MOD_ROWS - B - 1, D), F32)], axis=0)
    mod_all = _modulation(cc, w_mod, b_mod).reshape(DEPTH, MOD_ROWS, 1, 6 * D)

    wts = _prep_weights(w_in, q_norm, w_uq, kv_norm, w_uk, w_uv, sgu_ln_g, sgu_ln_b, w_spatial, b_spatial, w_pool,
                        pool_scale, w_fourier, w_out, ln1_g, ln1_b, w_ffn1, w_ffn3, w_ffn2, ln2_g, ln2_b)
    rope_tabs = _rope_tables(L)
    tabs_lat = _seq_tables(L)
    tabs_ctx = _seq_tables(Lc)

    for l in range(DEPTH):
        last = l == DEPTH - 1
        q, k, vt, loc_in, pool_in = _inproj(l, xs, mod_all, rope_tabs, wts, dims)
        attn = _attention_latent(q, k, vt, dims)
        loc = _local_mixers(l, loc_in, pool_in, tabs_lat, wts, L, 0, B)
        ctx_parts = None
        if not last:
            ctx_parts = (_attention_context(q, k, vt, dims),
                         _local_mixers(l, loc_in, pool_in, tabs_ctx, wts, Lc, r_lat // Lc, B))
        xs = (_tail(l, xs, attn, loc, ctx_parts, mod_all, wts, dims),)
    return xs[0].reshape(B, L, D)
```
